```python
import math
import jax, jax.numpy as jnp
from jax import lax
import numpy as np

D_MODEL = 1024
BATCH = 4
SEQ = 4096
DEPTH = 4

PLE_DIM = 256
N_A_LAYERS = DEPTH // 2
N_B_LAYERS = DEPTH - N_A_LAYERS
GDN_HEADS = 8
GDN_HEAD_DIM = 128
GDN_WIDTH = GDN_HEADS * GDN_HEAD_DIM
CONV_WIDTH = 4
GDN_CHUNK = 64
SB_HEADS = 8
SB_HEAD_DIM = 128
SB_WIDTH = SB_HEADS * SB_HEAD_DIM
SB_BLOCK = 128
FFN_HIDDEN = -(-8 * D_MODEL // (3 * 256)) * 256
EPS = 1e-6

kernel_name = "yoco_gdn_stickbreaking_hybrid"


def rms_norm(x, g):
    xf = x.astype(jnp.float32)
    y = xf * lax.rsqrt(jnp.mean(xf * xf, axis=-1, keepdims=True) + EPS)
    return (y * g.astype(jnp.float32)).astype(x.dtype)


def l2_norm(x):
    return x * lax.rsqrt(jnp.sum(x * x, axis=-1, keepdims=True) + EPS)


def causal_conv(x, w):
    k_w, c = w.shape
    return lax.conv_general_dilated(x, w[:, None, :], window_strides=(1,), padding=[(k_w - 1, 0)],
                                    dimension_numbers=('NWC', 'WIO', 'NWC'), feature_group_count=c)


def to_chunks(t):
    b, s, h = t.shape[:3]
    t = t.reshape((b, s // GDN_CHUNK, GDN_CHUNK, h) + t.shape[3:])
    return jnp.swapaxes(t, 2, 3)


def gated_delta_rule(q, k, v, g, beta):
    b_, s_, h_, dk = q.shape
    dv = v.shape[-1]
    c = GDN_CHUNK
    q, k, v, g, beta = to_chunks(q), to_chunks(k), to_chunks(v), to_chunks(g), to_chunks(beta)
    G = jnp.cumsum(g, axis=-1)
    causal = jnp.tril(jnp.ones((c, c), dtype=bool))
    strict = jnp.tril(jnp.ones((c, c), dtype=bool), -1)
    decay_mat = jnp.exp(jnp.where(causal, G[..., :, None] - G[..., None, :], -jnp.inf))
    kk = jnp.einsum('bnhrd,bnhsd->bnhrs', k, k)
    a_low = jnp.where(strict, beta[..., None] * kk * decay_mat, 0.0)
    eye = jnp.eye(c, dtype=q.dtype)
    rhs = jnp.concatenate([v * beta[..., None], k * (beta * jnp.exp(G))[..., None]], axis=-1)
    sol = lax.linalg.triangular_solve(a_low + eye, rhs, left_side=True, lower=True, unit_diagonal=True)
    u, w = sol[..., :dv], sol[..., dv:]
    qk = jnp.einsum('bnhrd,bnhsd->bnhrs', q, k) * decay_mat
    q_dec = q * jnp.exp(G)[..., None]
    k_dec = k * jnp.exp(G[..., -1:] - G)[..., None]
    g_last = jnp.exp(G[..., -1])

    def step(state, xs):
        q_c, k_c, qk_c, u_c, w_c, gl_c = xs
        v_new = u_c - jnp.einsum('bhcd,bhde->bhce', w_c, state)
        o = jnp.einsum('bhcd,bhde->bhce', q_c, state) + jnp.einsum('bhrs,bhse->bhre', qk_c, v_new)
        state = state * gl_c[..., None, None] + jnp.einsum('bhcd,bhce->bhde', k_c, v_new)
        return state, o

    xs = tuple(jnp.moveaxis(t, 1, 0) for t in (q_dec, k_dec, qk, u, w, g_last))
    s0 = jnp.zeros((b_, h_, dk, dv), q.dtype)
    _, o = lax.scan(step, s0, xs)
    return o.transpose(1, 0, 3, 2, 4).reshape(b_, s_, h_, dv)


def gdn_mixer(hn, w_in, w_conv, a_log, dt_bias, norm_g, w_out):
    b_, s_, _ = hn.shape
    proj = hn @ w_in
    qkv = jax.nn.silu(causal_conv(proj[..., :3 * GDN_WIDTH], w_conv))
    gate = proj[..., 3 * GDN_WIDTH:4 * GDN_WIDTH].reshape(b_, s_, GDN_HEADS, GDN_HEAD_DIM)
    a_in = proj[..., 4 * GDN_WIDTH:4 * GDN_WIDTH + GDN_HEADS].astype(jnp.float32)
    b_in = proj[..., 4 * GDN_WIDTH + GDN_HEADS:].astype(jnp.float32)
    qkv = qkv.astype(jnp.float32).reshape(b_, s_, 3, GDN_HEADS, GDN_HEAD_DIM)
    q = l2_norm(qkv[:, :, 0]) * (GDN_HEAD_DIM ** -0.5)
    k = l2_norm(qkv[:, :, 1])
    v = qkv[:, :, 2]
    beta = jax.nn.sigmoid(b_in)
    g = -jnp.exp(a_log.astype(jnp.float32)) * jax.nn.softplus(a_in + dt_bias.astype(jnp.float32))
    o = gated_delta_rule(q, k, v, g, beta).astype(hn.dtype)
    o = rms_norm(o, norm_g) * jax.nn.silu(gate)
    return o.reshape(b_, s_, GDN_WIDTH) @ w_out


def shared_kv(h, kv_norm, w_kv, k_norm):
    b_, s_, _ = h.shape
    kv = (rms_norm(h, kv_norm) @ w_kv).reshape(b_, s_, 2, SB_HEADS, SB_HEAD_DIM)
    k = rms_norm(kv[:, :, 0], k_norm).transpose(0, 2, 1, 3)
    v = kv[:, :, 1].transpose(0, 2, 1, 3)
    return k, v


def stick_breaking(q, k, v):
    s_ = q.shape[2]
    outs = []
    for blk in range(s_ // SB_BLOCK):
        t0, t1 = blk * SB_BLOCK, (blk + 1) * SB_BLOCK
        z = jnp.einsum('bhtd,bhsd->bhts', q[:, :, t0:t1], k[:, :, :t1]).astype(jnp.float32)
        t_idx = t0 + jnp.arange(SB_BLOCK)[:, None]
        s_idx = jnp.arange(t1)[None, :]
        mask = s_idx < t_idx
        log_not = jnp.where(mask, jax.nn.log_sigmoid(-z), 0.0)
        suffix = lax.cumsum(log_not, axis=3, reverse=True) - log_not
        log_w = jnp.where(mask, jax.nn.log_sigmoid(z) + suffix, -jnp.inf)
        wgt = jnp.exp(log_w).astype(v.dtype)
        outs.append(jnp.einsum('bhts,bhsd->bhtd', wgt, v[:, :, :t1]))
    return jnp.concatenate(outs, axis=2)


def sb_mixer(hn, k_sh, v_sh, w_q, q_norm, w_out):
    b_, s_, _ = hn.shape
    q = (hn @ w_q).reshape(b_, s_, SB_HEADS, SB_HEAD_DIM)
    q = (rms_norm(q, q_norm) * (SB_HEAD_DIM ** -0.5)).transpose(0, 2, 1, 3)
    o = stick_breaking(q, k_sh, v_sh)
    return o.transpose(0, 2, 1, 3).reshape(b_, s_, SB_WIDTH) @ w_out


def swiglu(hn, w_in, w_out):
    gu = hn @ w_in
    return (jax.nn.silu(gu[..., :FFN_HIDDEN]) * gu[..., FFN_HIDDEN:]) @ w_out


def setup_inputs(seed: int = 0) -> dict:
    key = jax.random.key(seed)
    ks = jax.random.split(key, 24)
    f32 = jnp.float32

    def nrm(k, shape, fan_in):
        return jax.random.normal(k, shape, f32) * (fan_in ** -0.5)

    def gain(k, shape):
        return 1.0 + 0.02 * jax.random.normal(k, shape, f32)

    dt = jnp.exp(jax.random.uniform(ks[8], (N_A_LAYERS, GDN_HEADS), f32, math.log(1e-3), math.log(1e-1)))
    return {
        "x": jax.random.normal(ks[0], (BATCH, SEQ, D_MODEL), f32),
        "p": jax.random.normal(ks[1], (DEPTH, BATCH, SEQ, PLE_DIM), f32),
        "ln_mix": gain(ks[2], (DEPTH, D_MODEL)),
        "ln_ffn": gain(ks[3], (DEPTH, D_MODEL)),
        "ln_ple": gain(ks[4], (DEPTH, D_MODEL)),
        "gdn_w_in": nrm(ks[5], (N_A_LAYERS, D_MODEL, 4 * GDN_WIDTH + 2 * GDN_HEADS), D_MODEL),
        "gdn_conv": nrm(ks[6], (N_A_LAYERS, CONV_WIDTH, 3 * GDN_WIDTH), CONV_WIDTH),
        "gdn_a_log": jnp.log(jax.random.uniform(ks[7], (N_A_LAYERS, GDN_HEADS), f32, 1.0, 16.0)),
        "gdn_dt_bias": dt + jnp.log(-jnp.expm1(-dt)),
        "gdn_norm": gain(ks[9], (N_A_LAYERS, GDN_HEAD_DIM)),
        "gdn_w_out": nrm(ks[10], (N_A_LAYERS, GDN_WIDTH, D_MODEL), GDN_WIDTH),
        "kv_norm": gain(ks[11], (D_MODEL,)),
        "w_kv": nrm(ks[12], (D_MODEL, 2 * SB_WIDTH), D_MODEL),
        "k_norm": gain(ks[13], (SB_HEAD_DIM,)),
        "sb_w_q": nrm(ks[14], (N_B_LAYERS, D_MODEL, SB_WIDTH), D_MODEL),
        "sb_q_norm": gain(ks[15], (N_B_LAYERS, SB_HEAD_DIM)),
        "sb_w_out": nrm(ks[16], (N_B_LAYERS, SB_WIDTH, D_MODEL), SB_WIDTH),
        "ffn_w_in": nrm(ks[17], (DEPTH, D_MODEL, 2 * FFN_HIDDEN), D_MODEL),
        "ffn_w_out": nrm(ks[18], (DEPTH, FFN_HIDDEN, D_MODEL), FFN_HIDDEN),
        "ple_w_proj": nrm(ks[19], (DEPTH, PLE_DIM, D_MODEL), PLE_DIM),
        "ple_w_gate": nrm(ks[20], (DEPTH, D_MODEL, D_MODEL), D_MODEL),
    }


def reference(x, p, ln_mix, ln_ffn, ln_ple, gdn_w_in, gdn_conv, gdn_a_log, gdn_dt_bias, gdn_norm,
              gdn_w_out, kv_norm, w_kv, k_norm, sb_w_q, sb_q_norm, sb_w_out, ffn_w_in, ffn_w_out,
              ple_w_proj, ple_w_gate):
    h = x
    k_sh, v_sh = None, None
    for i in range(DEPTH):
        hn = rms_norm(h, ln_mix[i])
        if i < N_A_LAYERS:
            h = h + gdn_mixer(hn, gdn_w_in[i], gdn_conv[i], gdn_a_log[i], gdn_dt_bias[i],
                              gdn_norm[i], gdn_w_out[i])
        else:
            j = i - N_A_LAYERS
            h = h + sb_mixer(hn, k_sh, v_sh, sb_w_q[j], sb_q_norm[j], sb_w_out[j])
        h = h + swiglu(rms_norm(h, ln_ffn[i]), ffn_w_in[i], ffn_w_out[i])
        h = h + (p[i] @ ple_w_proj[i]) * jax.nn.sigmoid(rms_norm(h, ln_ple[i]) @ ple_w_gate[i])
        if i == N_A_LAYERS - 1:
            k_sh, v_sh = shared_kv(h, kv_norm, w_kv, k_norm)
    return h
```

```python
import functools

import jax
import jax.numpy as jnp
from jax import lax
from jax.experimental import pallas as pl
from jax.experimental.pallas import tpu as pltpu

EPS = 1e-6
LANES = 128
HEADS = 8
CONV_W = 4
HALO = 8
VMEM_LIMIT = 56 * 1024 * 1024

F32 = jnp.float32
BF16 = jnp.bfloat16
HI = lax.Precision.HIGHEST


def _cparams(sem):
    return pltpu.CompilerParams(dimension_semantics=sem, vmem_limit_bytes=VMEM_LIMIT)


def _dot(a, b):
    return jnp.dot(a, b, preferred_element_type=F32)


def _dot_nt(a, b):
    return lax.dot_general(a, b, (((1,), (1,)), ((), ())), preferred_element_type=F32)


def _dot_hi(a, b):
    return jnp.dot(a, b, preferred_element_type=F32, precision=HI)


def _rms(x, g):
    return x * lax.rsqrt(jnp.mean(x * x, axis=-1, keepdims=True) + EPS) * g


def _sigmoid(x):
    return 1.0 / (1.0 + jnp.exp(-x))


def _norm_mm_kernel(x_ref, g_ref, w_ref, hg_ref, o_ref, xn_ref, *, n_normed_tiles, out_scale):
    j = pl.program_id(1)

    @pl.when(j == 0)
    def _():
        xn_ref[...] = _rms(x_ref[...], g_ref[...]).astype(BF16)

    y = _dot(xn_ref[...], w_ref[...])

    def head_normed():
        hg = hg_ref[...]
        for c in range(y.shape[1] // LANES):
            yc = y[:, c * LANES:(c + 1) * LANES]
            o_ref[:, c * LANES:(c + 1) * LANES] = (_rms(yc, hg) * out_scale).astype(o_ref.dtype)

    if n_normed_tiles == 0:
        o_ref[...] = y.astype(o_ref.dtype)
    else:
        pl.when(j < n_normed_tiles)(head_normed)

        @pl.when(j >= n_normed_tiles)
        def _():
            o_ref[...] = y.astype(o_ref.dtype)


def norm_mm(x, g, w, *, tm, tn, out_dtype, head_g=None, n_normed_tiles=0, out_scale=1.0):
    m, d = x.shape
    n = w.shape[1]
    if head_g is None:
        head_g = jnp.ones((1, LANES), F32)
    kern = functools.partial(_norm_mm_kernel, n_normed_tiles=n_normed_tiles, out_scale=out_scale)
    return pl.pallas_call(
        kern,
        grid=(m // tm, n // tn),
        in_specs=[
            pl.BlockSpec((tm, d), lambda i, j: (i, 0)),
            pl.BlockSpec((1, d), lambda i, j: (0, 0)),
            pl.BlockSpec((d, tn), lambda i, j: (0, j)),
            pl.BlockSpec((1, LANES), lambda i, j: (0, 0)),
        ],
        out_specs=pl.BlockSpec((tm, tn), lambda i, j: (i, j)),
        out_shape=jax.ShapeDtypeStruct((m, n), out_dtype),
        scratch_shapes=[pltpu.VMEM((tm, d), BF16)],
        compiler_params=_cparams(("parallel", "arbitrary")),
        name="norm_mm",
    )(x, g.reshape(1, d), w, head_g.reshape(1, LANES))


def _mm_res_kernel(a_ref, w_ref, r_ref, o_ref):
    o_ref[...] = r_ref[...] + _dot(a_ref[...].astype(BF16), w_ref[...])


def mm_res(a, w, res, *, tm):
    m, k = a.shape
    n = w.shape[1]
    return pl.pallas_call(
        _mm_res_kernel,
        grid=(m // tm,),
        in_specs=[
            pl.BlockSpec((tm, k), lambda i: (i, 0)),
            pl.BlockSpec((k, n), lambda i: (0, 0)),
            pl.BlockSpec((tm, n), lambda i: (i, 0)),
        ],
        out_specs=pl.BlockSpec((tm, n), lambda i: (i, 0)),
        out_shape=jax.ShapeDtypeStruct((m, n), F32),
        compiler_params=_cparams(("parallel",)),
        name="mm_res",
    )(a, w, res)


def _ffn_kernel(h_ref, g_ref, wa_ref, wb_ref, w2_ref, o_ref, hn_ref, acc_ref):
    f = pl.program_id(1)

    @pl.when(f == 0)
    def _():
        hn_ref[...] = _rms(h_ref[...], g_ref[...]).astype(BF16)
        acc_ref[...] = jnp.zeros_like(acc_ref)

    hn = hn_ref[...]
    a = _dot(hn, wa_ref[...])
    b = _dot(hn, wb_ref[...])
    act = (a * _sigmoid(a) * b).astype(BF16)
    acc_ref[...] += _dot(act, w2_ref[...])

    @pl.when(f == pl.num_programs(1) - 1)
    def _():
        o_ref[...] = h_ref[...] + acc_ref[...]


def ffn(h, g, w_in, w_out, *, tm, tf):
    m, d = h.shape
    hidden = w_out.shape[0]
    nf = hidden // tf
    return pl.pallas_call(
        _ffn_kernel,
        grid=(m // tm, nf),
        in_specs=[
            pl.BlockSpec((tm, d), lambda i, f: (i, 0)),
            pl.BlockSpec((1, d), lambda i, f: (0, 0)),
            pl.BlockSpec((d, tf), lambda i, f: (0, f)),
            pl.BlockSpec((d, tf), lambda i, f: (0, nf + f)),
            pl.BlockSpec((tf, d), lambda i, f: (f, 0)),
        ],
        out_specs=pl.BlockSpec((tm, d), lambda i, f: (i, 0)),
        out_shape=jax.ShapeDtypeStruct((m, d), F32),
        scratch_shapes=[pltpu.VMEM((tm, d), BF16), pltpu.VMEM((tm, d), F32)],
        compiler_params=_cparams(("parallel", "arbitrary")),
        name="ffn",
    )(h, g.reshape(1, d), w_in, w_in, w_out)


def _ple_kernel(h_ref, p_ref, g_ref, wp_ref, wg_ref, o_ref):
    h = h_ref[...]
    hn = _rms(h, g_ref[...]).astype(BF16)
    gate = _sigmoid(_dot(hn, wg_ref[...]))
    emb = _dot(p_ref[...].astype(BF16), wp_ref[...])
    o_ref[...] = h + emb * gate


def ple(h, p, g, wp, wg, *, tm):
    m, d = h.shape
    pd = p.shape[1]
    return pl.pallas_call(
        _ple_kernel,
        grid=(m // tm,),
        in_specs=[
            pl.BlockSpec((tm, d), lambda i: (i, 0)),
            pl.BlockSpec((tm, pd), lambda i: (i, 0)),
            pl.BlockSpec((1, d), lambda i: (0, 0)),
            pl.BlockSpec((pd, d), lambda i: (0, 0)),
            pl.BlockSpec((d, d), lambda i: (0, 0)),
        ],
        out_specs=pl.BlockSpec((tm, d), lambda i: (i, 0)),
        out_shape=jax.ShapeDtypeStruct((m, d), F32),
        compiler_params=_cparams(("parallel",)),
        name="ple",
    )(h, p, g.reshape(1, d), wp, wg)


TRI_BASE_LOG2 = 4


def _tri_masks(row, col):
    c = row.shape[0]
    eye = (row == col).astype(F32)
    diag = (row >> TRI_BASE_LOG2) == (col >> TRI_BASE_LOG2)
    levels = []
    sh = TRI_BASE_LOG2
    while (1 << sh) < c:
        rb = row >> sh
        levels.append(((rb & 1) == 1) & ((col >> sh) == rb - 1))
        sh += 1
    return eye, diag, levels


def _tri_inverse(a_low, masks):
    eye, diag, levels = masks
    x = jnp.where(diag, -a_low, 0.0)
    t = eye + x
    p = x
    for _ in range(TRI_BASE_LOG2 - 1):
        p = _dot_hi(p, p)
        t = t + _dot_hi(t, p)
    for lvl in levels:
        t = t - _dot_hi(t, _dot_hi(jnp.where(lvl, a_low, 0.0), t))
    return t


def _gdn_kernel(qkv_ref, gate_ref, ab_ref, conv_ref, alog_ref, dtb_ref, ng_ref, o_ref,
                state_ref, halo_ref):
    i = pl.program_id(1)
    c = qkv_ref.shape[1]
    width = HEADS * LANES

    @pl.when(i == 0)
    def _():
        state_ref[...] = jnp.zeros_like(state_ref)
        halo_ref[...] = jnp.zeros_like(halo_ref)

    row = lax.broadcasted_iota(jnp.int32, (c, c), 0)
    col = lax.broadcasted_iota(jnp.int32, (c, c), 1)
    causal = row >= col
    strict = row > col
    tri_incl = causal.astype(F32)
    tri_masks = _tri_masks(row, col)

    ab = ab_ref[0]
    x_dt = ab + dtb_ref[...]
    softplus = jnp.maximum(x_dt, 0.0) + jnp.log(1.0 + jnp.exp(-jnp.abs(x_dt)))
    g_all = -jnp.exp(alog_ref[...]) * softplus
    beta_all = _sigmoid(ab)
    gc_all = _dot_hi(tri_incl, g_all)
    gc_t = gc_all.T
    g_last = gc_all[c - 1:c, :]
    eg_all = jnp.exp(gc_all)
    egl_all = jnp.exp(g_last - gc_all)
    gl_all = jnp.exp(g_last)

    ng = ng_ref[...]

    def conv_silu(sec, h):
        lo = sec * width + h * LANES
        x = qkv_ref[0, :, lo:lo + LANES]
        xe = jnp.concatenate([halo_ref[:, lo:lo + LANES], x], axis=0)
        y = conv_ref[CONV_W - 1:CONV_W, lo:lo + LANES] * x
        for k in range(1, CONV_W):
            y = y + conv_ref[CONV_W - 1 - k:CONV_W - k, lo:lo + LANES] * pltpu.roll(xe, k, 0)[HALO:, :]
        return y * _sigmoid(y)

    for h in range(HEADS):
        q = conv_silu(0, h)
        k = conv_silu(1, h)
        v = conv_silu(2, h)
        q = q * lax.rsqrt(jnp.sum(q * q, axis=-1, keepdims=True) + EPS) * (LANES ** -0.5)
        k = k * lax.rsqrt(jnp.sum(k * k, axis=-1, keepdims=True) + EPS)

        beta = beta_all[:, HEADS + h:HEADS + h + 1]
        gcol = gc_all[:, h:h + 1]
        grow = gc_t[h:h + 1, :]
        decay = jnp.where(causal, jnp.exp(jnp.where(causal, gcol - grow, 0.0)), 0.0)
        eg = eg_all[:, h:h + 1]
        egl = egl_all[:, h:h + 1]
        gl = gl_all[:, h:h + 1]

        kb = k.astype(BF16)
        kk = _dot_nt(kb, kb)
        a_low = jnp.where(strict, beta * kk * decay, 0.0)
        t = _tri_inverse(a_low, tri_masks)
        rhs = jnp.concatenate([v * beta, k * (beta * eg)], axis=1)
        sol = _dot_hi(t, rhs)
        u = sol[:, :LANES]
        w = sol[:, LANES:]

        qk = _dot_nt(q.astype(BF16), kb) * decay
        q_dec = (q * eg).astype(BF16)
        k_dec_t = (k * egl).T.astype(BF16)

        s = state_ref[h]
        sb = s.astype(BF16)
        v_new = u - _dot(w.astype(BF16), sb)
        vb = v_new.astype(BF16)
        o = _dot(q_dec, sb) + _dot(qk.astype(BF16), vb)
        state_ref[h] = s * gl + _dot(k_dec_t, vb)

        gate = gate_ref[0, :, h * LANES:(h + 1) * LANES]
        o_ref[0, :, h * LANES:(h + 1) * LANES] = _rms(o, ng) * (gate * _sigmoid(gate))

    halo_ref[...] = qkv_ref[0, c - HALO:, :]


def gdn_core(proj, conv_w, a_log, dt_bias, norm_g, *, chunk):
    b, s, _ = proj.shape
    width = HEADS * LANES
    pad = lambda v: jnp.zeros((1, LANES), F32).at[0, :HEADS].set(v)
    return pl.pallas_call(
        _gdn_kernel,
        grid=(b, s // chunk),
        in_specs=[
            pl.BlockSpec((1, chunk, 3 * width), lambda bi, i: (bi, i, 0)),
            pl.BlockSpec((1, chunk, width), lambda bi, i: (bi, i, 3)),
            pl.BlockSpec((1, chunk, LANES), lambda bi, i: (bi, i, 4 * HEADS)),
            pl.BlockSpec((CONV_W, 3 * width), lambda bi, i: (0, 0)),
            pl.BlockSpec((1, LANES), lambda bi, i: (0, 0)),
            pl.BlockSpec((1, LANES), lambda bi, i: (0, 0)),
            pl.BlockSpec((1, LANES), lambda bi, i: (0, 0)),
        ],
        out_specs=pl.BlockSpec((1, chunk, width), lambda bi, i: (bi, i, 0)),
        out_shape=jax.ShapeDtypeStruct((b, s, width), F32),
        scratch_shapes=[pltpu.VMEM((HEADS, LANES, LANES), F32), pltpu.VMEM((HALO, 3 * width), F32)],
        compiler_params=_cparams(("parallel", "arbitrary")),
        name="gdn_core",
    )(proj, proj, proj, conv_w, pad(a_log), pad(dt_bias), norm_g.reshape(1, LANES))


def _sb_block(q, kb, vb, upper, carry, mask):
    z = _dot_nt(q, kb)
    lse = jnp.log(1.0 + jnp.exp(-jnp.abs(z)))
    softplus = jnp.maximum(z, 0.0) + lse
    log_sig = z - softplus
    log_not = -softplus
    if mask is not None:
        log_not = jnp.where(mask, log_not, 0.0)
    hi = log_not.astype(BF16)
    lo = (log_not - hi.astype(F32)).astype(BF16)
    suffix = _dot(hi, upper) + _dot(lo, upper)
    log_w = log_sig + suffix + carry
    if mask is not None:
        log_w = jnp.where(mask, log_w, -jnp.inf)
    wgt = jnp.exp(log_w).astype(BF16)
    total = suffix[:, 0:1] + log_not[:, 0:1]
    return _dot(wgt, vb), total


def _sb_kernel(q_ref, k_ref, v_ref, o_ref):
    qi = pl.program_id(2)
    tq = q_ref.shape[1]
    q = q_ref[0]
    row = lax.broadcasted_iota(jnp.int32, (tq, tq), 0)
    col = lax.broadcasted_iota(jnp.int32, (tq, tq), 1)
    upper = (row > col).astype(BF16)

    def kv(j):
        start = pl.multiple_of(j * tq, tq)
        return k_ref[0, pl.ds(start, tq), :], v_ref[0, pl.ds(start, tq), :]

    kd, vd = kv(qi)
    acc, carry = _sb_block(q, kd, vd, upper, jnp.zeros((tq, 1), F32), col < row)

    def body(n, state):
        acc, carry = state
        kb, vb = kv(qi - 1 - n)
        out, total = _sb_block(q, kb, vb, upper, carry, None)
        return acc + out, carry + total

    acc, _ = lax.fori_loop(0, qi, body, (acc, carry))
    o_ref[0] = acc


def sb_attention(q, kv, *, tq):
    b, s, width = q.shape
    return pl.pallas_call(
        _sb_kernel,
        grid=(b, HEADS, s // tq),
        in_specs=[
            pl.BlockSpec((1, tq, LANES), lambda bi, h, i: (bi, i, h)),
            pl.BlockSpec((1, s, LANES), lambda bi, h, i: (bi, 0, h)),
            pl.BlockSpec((1, s, LANES), lambda bi, h, i: (bi, 0, HEADS + h)),
        ],
        out_specs=pl.BlockSpec((1, tq, LANES), lambda bi, h, i: (bi, i, h)),
        out_shape=jax.ShapeDtypeStruct((b, s, width), F32),
        compiler_params=_cparams(("parallel", "parallel", "arbitrary")),
        name="sb_attention",
    )(q, kv, kv)


def kernel(x, p, ln_mix, ln_ffn, ln_ple, gdn_w_in, gdn_conv, gdn_a_log, gdn_dt_bias, gdn_norm,
           gdn_w_out, kv_norm, w_kv, k_norm, sb_w_q, sb_q_norm, sb_w_out, ffn_w_in, ffn_w_out,
           ple_w_proj, ple_w_gate):
    b, s, d = x.shape
    depth = p.shape[0]
    n_a = gdn_w_in.shape[0]
    m = b * s
    tm = 512
    width = HEADS * LANES

    h = x.reshape(m, d)
    kv = None
    for i in range(depth):
        if i < n_a:
            w_in = gdn_w_in[i]
            n_proj = w_in.shape[1]
            n_pad = -(-n_proj // (3 * LANES)) * (3 * LANES)
            w_in = jnp.pad(w_in, ((0, 0), (0, n_pad - n_proj))).astype(BF16)
            proj = norm_mm(h, ln_mix[i], w_in, tm=tm, tn=n_pad // 3, out_dtype=F32)
            o = gdn_core(proj.reshape(b, s, n_pad), gdn_conv[i], gdn_a_log[i], gdn_dt_bias[i],
                         gdn_norm[i], chunk=LANES)
            h = mm_res(o.reshape(m, width), gdn_w_out[i].astype(BF16), h, tm=tm)
        else:
            j = i - n_a
            q = norm_mm(h, ln_mix[i], sb_w_q[j].astype(BF16), tm=tm, tn=width, out_dtype=BF16,
                        head_g=sb_q_norm[j], n_normed_tiles=1, out_scale=LANES ** -0.5)
            o = sb_attention(q.reshape(b, s, width), kv, tq=LANES)
            h = mm_res(o.reshape(m, width), sb_w_out[j].astype(BF16), h, tm=tm)
        h = ffn(h, ln_ffn[i], ffn_w_in[i].astype(BF16), ffn_w_out[i].astype(BF16), tm=tm, tf=256)
        h = ple(h, p[i].reshape(m, -1), ln_ple[i], ple_w_proj[i].astype(BF16),
                ple_w_gate[i].astype(BF16), tm=tm)
        if i == n_a - 1:
            kv = norm_mm(h, kv_norm, w_kv.astype(BF16), tm=tm, tn=width, out_dtype=BF16,
                         head_g=k_norm, n_normed_tiles=1).reshape(b, s, 2 * width)
    return h.reshape(b, s, d)
```

```python
import functools

import jax
import jax.numpy as jnp
from jax import lax
from jax.experimental import pallas as pl
from jax.experimental.pallas import tpu as pltpu

EPS = 1e-6
LANES = 128
HEADS = 8
CONV_W = 4
HALO = 8
VMEM_LIMIT = 56 * 1024 * 1024

F32 = jnp.float32
BF16 = jnp.bfloat16
HI = lax.Precision.HIGHEST


def _cparams(sem):
    return pltpu.CompilerParams(dimension_semantics=sem, vmem_limit_bytes=VMEM_LIMIT)


def _dot(a, b):
    return jnp.dot(a, b, preferred_element_type=F32)


def _dot_nt(a, b):
    return lax.dot_general(a, b, (((1,), (1,)), ((), ())), preferred_element_type=F32)


def _dot_hi(a, b):
    return jnp.dot(a, b, preferred_element_type=F32, precision=HI)


def _rms(x, g):
    return x * lax.rsqrt(jnp.mean(x * x, axis=-1, keepdims=True) + EPS) * g


def _sigmoid(x):
    return 1.0 / (1.0 + jnp.exp(-x))


def _norm_mm_kernel(x_ref, g_ref, w_ref, hg_ref, o_ref, xn_ref, *, n_normed_tiles, out_scale):
    j = pl.program_id(1)

    @pl.when(j == 0)
    def _():
        xn_ref[...] = _rms(x_ref[...], g_ref[...]).astype(BF16)

    y = _dot(xn_ref[...], w_ref[...])

    def head_normed():
        hg = hg_ref[...]
        for c in range(y.shape[1] // LANES):
            yc = y[:, c * LANES:(c + 1) * LANES]
            o_ref[:, c * LANES:(c + 1) * LANES] = (_rms(yc, hg) * out_scale).astype(o_ref.dtype)

    if n_normed_tiles == 0:
        o_ref[...] = y.astype(o_ref.dtype)
    else:
        pl.when(j < n_normed_tiles)(head_normed)

        @pl.when(j >= n_normed_tiles)
        def _():
            o_ref[...] = y.astype(o_ref.dtype)


def norm_mm(x, g, w, *, tm, tn, out_dtype, head_g=None, n_normed_tiles=0, out_scale=1.0):
    m, d = x.shape
    n = w.shape[1]
    if head_g is None:
        head_g = jnp.ones((1, LANES), F32)
    kern = functools.partial(_norm_mm_kernel, n_normed_tiles=n_normed_tiles, out_scale=out_scale)
    return pl.pallas_call(
        kern,
        grid=(m // tm, n // tn),
        in_specs=[
            pl.BlockSpec((tm, d), lambda i, j: (i, 0)),
            pl.BlockSpec((1, d), lambda i, j: (0, 0)),
            pl.BlockSpec((d, tn), lambda i, j: (0, j)),
            pl.BlockSpec((1, LANES), lambda i, j: (0, 0)),
        ],
        out_specs=pl.BlockSpec((tm, tn), lambda i, j: (i, j)),
        out_shape=jax.ShapeDtypeStruct((m, n), out_dtype),
        scratch_shapes=[pltpu.VMEM((tm, d), BF16)],
        compiler_params=_cparams(("parallel", "arbitrary")),
        name="norm_mm",
    )(x, g.reshape(1, d), w, head_g.reshape(1, LANES))


def _mm_res_kernel(a_ref, w_ref, r_ref, o_ref):
    o_ref[...] = r_ref[...] + _dot(a_ref[...].astype(BF16), w_ref[...])


def mm_res(a, w, res, *, tm):
    m, k = a.shape
    n = w.shape[1]
    return pl.pallas_call(
        _mm_res_kernel,
        grid=(m // tm,),
        in_specs=[
            pl.BlockSpec((tm, k), lambda i: (i, 0)),
            pl.BlockSpec((k, n), lambda i: (0, 0)),
            pl.BlockSpec((tm, n), lambda i: (i, 0)),
        ],
        out_specs=pl.BlockSpec((tm, n), lambda i: (i, 0)),
        out_shape=jax.ShapeDtypeStruct((m, n), F32),
        compiler_params=_cparams(("parallel",)),
        name="mm_res",
    )(a, w, res)


def _ffn_kernel(h_ref, g_ref, wa_ref, wb_ref, w2_ref, o_ref, hn_ref, acc_ref):
    f = pl.program_id(1)

    @pl.when(f == 0)
    def _():
        hn_ref[...] = _rms(h_ref[...], g_ref[...]).astype(BF16)
        acc_ref[...] = jnp.zeros_like(acc_ref)

    hn = hn_ref[...]
    a = _dot(hn, wa_ref[...])
    b = _dot(hn, wb_ref[...])
    act = (a * _sigmoid(a) * b).astype(BF16)
    acc_ref[...] += _dot(act, w2_ref[...])

    @pl.when(f == pl.num_programs(1) - 1)
    def _():
        o_ref[...] = h_ref[...] + acc_ref[...]


def ffn(h, g, w_in, w_out, *, tm, tf):
    m, d = h.shape
    hidden = w_out.shape[0]
    nf = hidden // tf
    return pl.pallas_call(
        _ffn_kernel,
        grid=(m // tm, nf),
        in_specs=[
            pl.BlockSpec((tm, d), lambda i, f: (i, 0)),
            pl.BlockSpec((1, d), lambda i, f: (0, 0)),
            pl.BlockSpec((d, tf), lambda i, f: (0, f)),
            pl.BlockSpec((d, tf), lambda i, f: (0, nf + f)),
            pl.BlockSpec((tf, d), lambda i, f: (f, 0)),
        ],
        out_specs=pl.BlockSpec((tm, d), lambda i, f: (i, 0)),
        out_shape=jax.ShapeDtypeStruct((m, d), F32),
        scratch_shapes=[pltpu.VMEM((tm, d), BF16), pltpu.VMEM((tm, d), F32)],
        compiler_params=_cparams(("parallel", "arbitrary")),
        name="ffn",
    )(h, g.reshape(1, d), w_in, w_in, w_out)


def _ple_kernel(h_ref, p_ref, g_ref, wp_ref, wg_ref, o_ref):
    h = h_ref[...]
    hn = _rms(h, g_ref[...]).astype(BF16)
    gate = _sigmoid(_dot(hn, wg_ref[...]))
    emb = _dot(p_ref[...].astype(BF16), wp_ref[...])
    o_ref[...] = h + emb * gate


def ple(h, p, g, wp, wg, *, tm):
    m, d = h.shape
    pd = p.shape[1]
    return pl.pallas_call(
        _ple_kernel,
        grid=(m // tm,),
        in_specs=[
            pl.BlockSpec((tm, d), lambda i: (i, 0)),
            pl.BlockSpec((tm, pd), lambda i: (i, 0)),
            pl.BlockSpec((1, d), lambda i: (0, 0)),
            pl.BlockSpec((pd, d), lambda i: (0, 0)),
            pl.BlockSpec((d, d), lambda i: (0, 0)),
        ],
        out_specs=pl.BlockSpec((tm, d), lambda i: (i, 0)),
        out_shape=jax.ShapeDtypeStruct((m, d), F32),
        compiler_params=_cparams(("parallel",)),
        name="ple",
    )(h, p, g.reshape(1, d), wp, wg)


TRI_BASE_LOG2 = 4


def _tri_masks(row, col):
    c = row.shape[0]
    eye = (row == col).astype(F32)
    diag = (row >> TRI_BASE_LOG2) == (col >> TRI_BASE_LOG2)
    levels = []
    sh = TRI_BASE_LOG2
    while (1 << sh) < c:
        rb = row >> sh
        levels.append(((rb & 1) == 1) & ((col >> sh) == rb - 1))
        sh += 1
    return eye, diag, levels


def _bdot(a, b):
    return _dot(a.astype(BF16), b.astype(BF16))


def _split_bf16(x):
    hi = x.astype(BF16)
    return hi, (x - hi.astype(F32)).astype(BF16)


def _gdn_kernel(qkv_ref, gate_ref, ab_ref, conv_ref, alog_ref, dtb_ref, ng_ref, o_ref,
                state_ref, halo_ref):
    i = pl.program_id(1)
    c = qkv_ref.shape[1]
    width = HEADS * LANES
    heads = range(HEADS)

    @pl.when(i == 0)
    def _():
        state_ref[...] = jnp.zeros_like(state_ref)
        halo_ref[...] = jnp.zeros_like(halo_ref)

    row = lax.broadcasted_iota(jnp.int32, (c, c), 0)
    col = lax.broadcasted_iota(jnp.int32, (c, c), 1)
    causal = row >= col
    strict = row > col
    eye, diag, levels = _tri_masks(row, col)

    ab = ab_ref[0]
    x_dt = ab + dtb_ref[...]
    softplus = jnp.maximum(x_dt, 0.0) + jnp.log(1.0 + jnp.exp(-jnp.abs(x_dt)))
    g_all = -jnp.exp(alog_ref[...]) * softplus
    beta_all = _sigmoid(ab)
    g_hi, g_lo = _split_bf16(g_all)
    tri_incl = causal.astype(BF16)
    gc_all = _dot(tri_incl, g_hi) + _dot(tri_incl, g_lo)
    gc_t = gc_all.T
    g_last = gc_all[c - 1:c, :]
    eg_all = jnp.exp(gc_all)
    egl_all = jnp.exp(g_last - gc_all)
    gl_all = jnp.exp(g_last)

    def conv_silu(sec, h):
        lo = sec * width + h * LANES
        x = qkv_ref[0, :, lo:lo + LANES]
        xe = jnp.concatenate([halo_ref[:, lo:lo + LANES], x], axis=0)
        y = conv_ref[CONV_W - 1:CONV_W, lo:lo + LANES] * x
        for k in range(1, CONV_W):
            y = y + conv_ref[CONV_W - 1 - k:CONV_W - k, lo:lo + LANES] * pltpu.roll(xe, k, 0)[HALO:, :]
        return y * _sigmoid(y)

    q, k, v, decay = [], [], [], []
    for h in heads:
        qh = conv_silu(0, h)
        kh = conv_silu(1, h)
        q.append(qh * lax.rsqrt(jnp.sum(qh * qh, axis=-1, keepdims=True) + EPS) * (LANES ** -0.5))
        k.append(kh * lax.rsqrt(jnp.sum(kh * kh, axis=-1, keepdims=True) + EPS))
        v.append(conv_silu(2, h))
        diff = jnp.where(causal, gc_all[:, h:h + 1] - gc_t[h:h + 1, :], 0.0)
        decay.append(jnp.where(causal, jnp.exp(diff), 0.0))
    beta = [beta_all[:, HEADS + h:HEADS + h + 1] for h in heads]
    eg = [eg_all[:, h:h + 1] for h in heads]

    qkk = [_dot_nt(jnp.concatenate([q[h], k[h]], axis=0).astype(BF16), k[h].astype(BF16)) for h in heads]
    qk = [(qkk[h][:c] * decay[h]).astype(BF16) for h in heads]
    a_low = [jnp.where(strict, beta[h] * qkk[h][c:] * decay[h], 0.0) for h in heads]

    pw = [jnp.where(diag, -a_low[h], 0.0) for h in heads]
    t = [eye + pw[h] for h in heads]
    for _ in range(TRI_BASE_LOG2 - 1):
        pw = [_bdot(pw[h], pw[h]) for h in heads]
        t = [t[h] + _bdot(t[h], pw[h]) for h in heads]
    for lvl in levels:
        y = [_bdot(jnp.where(lvl, a_low[h], 0.0), t[h]) for h in heads]
        t = [t[h] - _bdot(t[h], y[h]) for h in heads]

    rhs = [jnp.concatenate([v[h] * beta[h], k[h] * (beta[h] * eg[h])], axis=1) for h in heads]
    sol = [_bdot(t[h], rhs[h]) for h in heads]

    sb = [state_ref[h].astype(BF16) for h in heads]
    lhs1 = [jnp.concatenate([sol[h][:, LANES:], q[h] * eg[h]], axis=0).astype(BF16) for h in heads]
    r1 = [_dot(lhs1[h], sb[h]) for h in heads]
    vb = [(sol[h][:, :LANES] - r1[h][:c]).astype(BF16) for h in heads]
    lhs2 = [jnp.concatenate([qk[h], (k[h] * egl_all[:, h:h + 1]).T.astype(BF16)], axis=0) for h in heads]
    r2 = [_dot(lhs2[h], vb[h]) for h in heads]

    ng = ng_ref[...]
    for h in heads:
        state_ref[h] = state_ref[h] * gl_all[:, h:h + 1] + r2[h][c:]
        o = r1[h][c:] + r2[h][:c]
        gate = gate_ref[0, :, h * LANES:(h + 1) * LANES]
        o_ref[0, :, h * LANES:(h + 1) * LANES] = _rms(o, ng) * (gate * _sigmoid(gate))

    halo_ref[...] = qkv_ref[0, c - HALO:, :]


def gdn_core(proj, conv_w, a_log, dt_bias, norm_g, *, chunk):
    b, s, _ = proj.shape
    width = HEADS * LANES
    pad = lambda v: jnp.zeros((1, LANES), F32).at[0, :HEADS].set(v)
    return pl.pallas_call(
        _gdn_kernel,
        grid=(b, s // chunk),
        in_specs=[
            pl.BlockSpec((1, chunk, 3 * width), lambda bi, i: (bi, i, 0)),
            pl.BlockSpec((1, chunk, width), lambda bi, i: (bi, i, 3)),
            pl.BlockSpec((1, chunk, LANES), lambda bi, i: (bi, i, 4 * HEADS)),
            pl.BlockSpec((CONV_W, 3 * width), lambda bi, i: (0, 0)),
            pl.BlockSpec((1, LANES), lambda bi, i: (0, 0)),
            pl.BlockSpec((1, LANES), lambda bi, i: (0, 0)),
            pl.BlockSpec((1, LANES), lambda bi, i: (0, 0)),
        ],
        out_specs=pl.BlockSpec((1, chunk, width), lambda bi, i: (bi, i, 0)),
        out_shape=jax.ShapeDtypeStruct((b, s, width), F32),
        scratch_shapes=[pltpu.VMEM((HEADS, LANES, LANES), F32), pltpu.VMEM((HALO, 3 * width), F32)],
        compiler_params=_cparams(("parallel", "arbitrary")),
        name="gdn_core",
    )(proj, proj, proj, conv_w, pad(a_log), pad(dt_bias), norm_g.reshape(1, LANES))


def _sb_blocks(qs, kbs, vbs, upper, carries, mask):
    n = range(len(qs))
    z = [_dot_nt(qs[i], kbs[i]) for i in n]
    log_sig, log_not = [], []
    for i in n:
        softplus = jnp.maximum(z[i], 0.0) + jnp.log(1.0 + jnp.exp(-jnp.abs(z[i])))
        log_sig.append(z[i] - softplus)
        log_not.append(-softplus if mask is None else jnp.where(mask, -softplus, 0.0))
    split = [_split_bf16(log_not[i]) for i in n]
    suffix = [_dot(split[i][0], upper) + _dot(split[i][1], upper) for i in n]
    wgt = []
    for i in n:
        log_w = log_sig[i] + suffix[i] + carries[i]
        if mask is not None:
            log_w = jnp.where(mask, log_w, -jnp.inf)
        wgt.append(jnp.exp(log_w).astype(BF16))
    out = [_dot(wgt[i], vbs[i]) for i in n]
    total = [suffix[i][:, 0:1] + log_not[i][:, 0:1] for i in n]
    return out, total


SB_BLK = LANES
SB_STATIC_BANDS = 3
F32_EXP_UNDERFLOW = -104.0
NEG_BIG = -1e30


def _sb_kernel(q_ref, k_ref, v_ref, o_ref, carry_ref):
    gi = pl.program_id(2)
    n_sub = q_ref.shape[1] // SB_BLK
    first_blk = gi * n_sub
    row = lax.broadcasted_iota(jnp.int32, (SB_BLK, SB_BLK), 0)
    col = lax.broadcasted_iota(jnp.int32, (SB_BLK, SB_BLK), 1)
    upper = (row > col).astype(BF16)

    def rows(i):
        return slice(i * SB_BLK, (i + 1) * SB_BLK)

    def band(d, first):
        subs = range(n_sub)
        blk = [first_blk + i - d for i in subs]
        start = [pl.multiple_of(jnp.maximum(blk[i], 0) * SB_BLK, SB_BLK) for i in subs]
        qs = [q_ref[0, rows(i), :] for i in subs]
        kbs = [k_ref[0, pl.ds(start[i], SB_BLK), :] for i in subs]
        vbs = [v_ref[0, pl.ds(start[i], SB_BLK), :] for i in subs]
        if first:
            out, total = _sb_blocks(qs, kbs, vbs, upper, [0.0] * n_sub, col < row)
            for i in subs:
                o_ref[0, rows(i), :] = out[i]
                carry_ref[:, i:i + 1] = total[i]
        else:
            carries = [jnp.where(blk[i] >= 0, carry_ref[:, i:i + 1], NEG_BIG) for i in subs]
            out, total = _sb_blocks(qs, kbs, vbs, upper, carries, None)
            for i in subs:
                o_ref[0, rows(i), :] += out[i]
                carry_ref[:, i:i + 1] = carries[i] + total[i]

    band(0, True)
    for d in range(1, SB_STATIC_BANDS):
        band(d, False)

    def any_row_live():
        return jnp.max(carry_ref[:, 0:n_sub]) >= F32_EXP_UNDERFLOW

    def unfinished(state):
        d, live = state
        return jnp.logical_and(d <= first_blk + n_sub - 1, live)

    def body(state):
        d, _ = state
        band(d, False)
        return d + 1, any_row_live()

    lax.while_loop(unfinished, body, (jnp.int32(SB_STATIC_BANDS), any_row_live()))


def sb_attention(q, kv, *, tq):
    b, s, width = q.shape
    return pl.pallas_call(
        _sb_kernel,
        grid=(b, HEADS, s // tq),
        in_specs=[
            pl.BlockSpec((1, tq, LANES), lambda bi, h, i: (bi, i, h)),
            pl.BlockSpec((1, s, LANES), lambda bi, h, i: (bi, 0, h)),
            pl.BlockSpec((1, s, LANES), lambda bi, h, i: (bi, 0, HEADS + h)),
        ],
        out_specs=pl.BlockSpec((1, tq, LANES), lambda bi, h, i: (bi, i, h)),
        out_shape=jax.ShapeDtypeStruct((b, s, width), F32),
        scratch_shapes=[pltpu.VMEM((SB_BLK, LANES), F32)],
        compiler_params=_cparams(("parallel", "parallel", "arbitrary")),
        name="sb_attention",
    )(q, kv, kv)


def kernel(x, p, ln_mix, ln_ffn, ln_ple, gdn_w_in, gdn_conv, gdn_a_log, gdn_dt_bias, gdn_norm,
           gdn_w_out, kv_norm, w_kv, k_norm, sb_w_q, sb_q_norm, sb_w_out, ffn_w_in, ffn_w_out,
           ple_w_proj, ple_w_gate):
    b, s, d = x.shape
    depth = p.shape[0]
    n_a = gdn_w_in.shape[0]
    m = b * s
    tm = 512
    width = HEADS * LANES

    h = x.reshape(m, d)
    kv = None
    for i in range(depth):
        if i < n_a:
            w_in = gdn_w_in[i]
            n_proj = w_in.shape[1]
            n_pad = -(-n_proj // (3 * LANES)) * (3 * LANES)
            w_in = jnp.pad(w_in, ((0, 0), (0, n_pad - n_proj))).astype(BF16)
            proj = norm_mm(h, ln_mix[i], w_in, tm=tm, tn=n_pad // 3, out_dtype=F32)
            o = gdn_core(proj.reshape(b, s, n_pad), gdn_conv[i], gdn_a_log[i], gdn_dt_bias[i],
                         gdn_norm[i], chunk=LANES)
            h = mm_res(o.reshape(m, width), gdn_w_out[i].astype(BF16), h, tm=tm)
        else:
            j = i - n_a
            q = norm_mm(h, ln_mix[i], sb_w_q[j].astype(BF16), tm=tm, tn=width, out_dtype=BF16,
                        head_g=sb_q_norm[j], n_normed_tiles=1, out_scale=LANES ** -0.5)
            o = sb_attention(q.reshape(b, s, width), kv, tq=8 * SB_BLK)
            h = mm_res(o.reshape(m, width), sb_w_out[j].astype(BF16), h, tm=tm)
        h = ffn(h, ln_ffn[i], ffn_w_in[i].astype(BF16), ffn_w_out[i].astype(BF16), tm=tm, tf=256)
        h = ple(h, p[i].reshape(m, -1), ln_ple[i], ple_w_proj[i].astype(BF16),
                ple_w_gate[i].astype(BF16), tm=tm)
        if i == n_a - 1:
            kv = norm_mm(h, kv_norm, w_kv.astype(BF16), tm=tm, tn=width, out_dtype=BF16,
                         head_g=k_norm, n_normed_tiles=1).reshape(b, s, 2 * width)
    return h.reshape(b, s, d)
```

```python
import functools

import jax
import jax.numpy as jnp
from jax import lax
from jax.experimental import pallas as pl
from jax.experimental.pallas import tpu as pltpu

EPS = 1e-6
LANES = 128
HEADS = 8
CONV_W = 4
HALO = 16
VMEM_LIMIT = 56 * 1024 * 1024
TM = 512
FFN_CHUNK = 256

F32 = jnp.float32
BF16 = jnp.bfloat16
HI = lax.Precision.HIGHEST


def _cparams(sem):
    return pltpu.CompilerParams(dimension_semantics=sem, vmem_limit_bytes=VMEM_LIMIT)


def _dot(a, b):
    return jnp.dot(a, b, preferred_element_type=F32)


def _dot_nt(a, b):
    return lax.dot_general(a, b, (((1,), (1,)), ((), ())), preferred_element_type=F32)


def _dot_hi(a, b):
    return jnp.dot(a, b, preferred_element_type=F32, precision=HI)


def _rms(x, g):
    return x * lax.rsqrt(jnp.mean(x * x, axis=-1, keepdims=True) + EPS) * g


def _sigmoid(x):
    return 1.0 / (1.0 + jnp.exp(-x))


def _resident(shape):
    return pl.BlockSpec(shape, lambda i: (0,) * len(shape), pipeline_mode=pl.Buffered(1))


def _proj_kernel(x_ref, g_ref, w_ref, hg_ref, *rest, n_normed, out_scale, has_extra):
    if has_extra:
        wx_ref, o_ref, ox_ref = rest
    else:
        o_ref, = rest
    xn = _rms(x_ref[...], g_ref[...]).astype(BF16)
    hg = hg_ref[...]
    n = w_ref.shape[1]
    step = 4 * LANES
    for lo in range(0, n, step):
        y = _dot(xn, w_ref[:, lo:lo + step])
        for c in range(step // LANES):
            col = lo + c * LANES
            yc = y[:, c * LANES:(c + 1) * LANES]
            if col < n_normed:
                yc = _rms(yc, hg) * out_scale
            o_ref[:, col:col + LANES] = yc.astype(o_ref.dtype)
    if has_extra:
        ox_ref[...] = _dot(xn, wx_ref[...])


def proj(x, g, w, *, tm, head_g=None, n_normed=0, out_scale=1.0, w_extra=None):
    m, d = x.shape
    n = w.shape[1]
    if head_g is None:
        head_g = jnp.ones((LANES,), F32)
    has_extra = w_extra is not None
    kern = functools.partial(_proj_kernel, n_normed=n_normed, out_scale=out_scale, has_extra=has_extra)
    in_specs = [pl.BlockSpec((tm, d), lambda i: (i, 0)), _resident((1, d)), _resident((d, n)),
                _resident((1, LANES))]
    out_specs = [pl.BlockSpec((tm, n), lambda i: (i, 0))]
    out_shape = [jax.ShapeDtypeStruct((m, n), BF16)]
    args = [x, g.reshape(1, d), w, head_g.reshape(1, LANES)]
    if has_extra:
        nx = w_extra.shape[1]
        in_specs.append(_resident((d, nx)))
        out_specs.append(pl.BlockSpec((tm, nx), lambda i: (i, 0)))
        out_shape.append(jax.ShapeDtypeStruct((m, nx), F32))
        args.append(w_extra)
    out = pl.pallas_call(
        kern,
        grid=(m // tm,),
        in_specs=in_specs,
        out_specs=out_specs,
        out_shape=out_shape,
        compiler_params=_cparams(("parallel",)),
        name="proj",
    )(*args)
    return out if has_extra else out[0]


def _post_kernel(h_ref, o_ref, p_ref, wo_ref, gf_ref, w1_ref, w2_ref, gp_ref, wp_ref, wg_ref, out_ref,
                 hn_ref, acc_ref):
    nf, _, tf2 = w1_ref.shape
    tf = tf2 // 2
    h1 = h_ref[...] + _dot(o_ref[...], wo_ref[...])
    hn_ref[...] = _rms(h1, gf_ref[...]).astype(BF16)
    acc_ref[...] = h1
    for f in range(nf):
        ab = _dot(hn_ref[...], w1_ref[f])
        a = ab[:, :tf]
        act = (a * _sigmoid(a) * ab[:, tf:]).astype(BF16)
        acc_ref[...] += _dot(act, w2_ref[f])
    h2 = acc_ref[...]
    gate = _sigmoid(_dot(_rms(h2, gp_ref[...]).astype(BF16), wg_ref[...]))
    emb = _dot(p_ref[...].astype(BF16), wp_ref[...])
    out_ref[...] = h2 + emb * gate


def post_mixer(h, o, p, w_out, g_ffn, w1, w2, g_ple, wp, wg, *, tm):
    m, d = h.shape
    return pl.pallas_call(
        _post_kernel,
        grid=(m // tm,),
        in_specs=[
            pl.BlockSpec((tm, d), lambda i: (i, 0)),
            pl.BlockSpec((tm, o.shape[1]), lambda i: (i, 0)),
            pl.BlockSpec((tm, p.shape[1]), lambda i: (i, 0)),
            _resident(w_out.shape), _resident((1, d)), _resident(w1.shape), _resident(w2.shape),
            _resident((1, d)), _resident(wp.shape), _resident(wg.shape),
        ],
        out_specs=pl.BlockSpec((tm, d), lambda i: (i, 0)),
        out_shape=jax.ShapeDtypeStruct((m, d), F32),
        scratch_shapes=[pltpu.VMEM((tm, d), BF16), pltpu.VMEM((tm, d), F32)],
        compiler_params=_cparams(("parallel",)),
        name="post_mixer",
    )(h, o, p, w_out, g_ffn.reshape(1, d), w1, w2, g_ple.reshape(1, d), wp, wg)


def _ffn_weights(w_in, w_out, tf):
    d, two_f = w_in.shape
    hidden = two_f // 2
    nf = hidden // tf
    w1 = w_in.astype(BF16).reshape(d, 2, nf, tf).transpose(2, 0, 1, 3).reshape(nf, d, 2 * tf)
    return w1, w_out.astype(BF16).reshape(nf, tf, d)


TRI_BASE_LOG2 = 4


def _tri_masks(row, col):
    c = row.shape[0]
    eye = (row == col).astype(F32)
    diag = (row >> TRI_BASE_LOG2) == (col >> TRI_BASE_LOG2)
    levels = []
    sh = TRI_BASE_LOG2
    while (1 << sh) < c:
        rb = row >> sh
        levels.append(((rb & 1) == 1) & ((col >> sh) == rb - 1))
        sh += 1
    return eye, diag, levels


def _bdot(a, b):
    return _dot(a.astype(BF16), b.astype(BF16))


def _split_bf16(x):
    hi = x.astype(BF16)
    return hi, (x - hi.astype(F32)).astype(BF16)


def _gdn_kernel(qkv_ref, gate_ref, ab_ref, conv_ref, alog_ref, dtb_ref, ng_ref, o_ref,
                state_ref, halo_ref):
    i = pl.program_id(1)
    c = qkv_ref.shape[1]
    width = HEADS * LANES
    heads = range(HEADS)

    @pl.when(i == 0)
    def _():
        state_ref[...] = jnp.zeros_like(state_ref)
        halo_ref[...] = jnp.zeros_like(halo_ref)

    row = lax.broadcasted_iota(jnp.int32, (c, c), 0)
    col = lax.broadcasted_iota(jnp.int32, (c, c), 1)
    causal = row >= col
    strict = row > col
    eye, diag, levels = _tri_masks(row, col)

    ab = ab_ref[0]
    x_dt = ab + dtb_ref[...]
    softplus = jnp.maximum(x_dt, 0.0) + jnp.log(1.0 + jnp.exp(-jnp.abs(x_dt)))
    g_all = -jnp.exp(alog_ref[...]) * softplus
    beta_all = _sigmoid(ab)
    g_hi, g_lo = _split_bf16(g_all)
    tri_incl = causal.astype(BF16)
    gc_all = _dot(tri_incl, g_hi) + _dot(tri_incl, g_lo)
    gc_t = gc_all.T
    g_last = gc_all[c - 1:c, :]
    eg_all = jnp.exp(gc_all)
    egl_all = jnp.exp(g_last - gc_all)
    gl_all = jnp.exp(g_last)

    def conv_silu(sec, h):
        lo = sec * width + h * LANES
        x = qkv_ref[0, :, lo:lo + LANES].astype(F32)
        xe = jnp.concatenate([halo_ref[:, lo:lo + LANES], x], axis=0)
        y = conv_ref[CONV_W - 1:CONV_W, lo:lo + LANES] * x
        for k in range(1, CONV_W):
            y = y + conv_ref[CONV_W - 1 - k:CONV_W - k, lo:lo + LANES] * pltpu.roll(xe, k, 0)[HALO:, :]
        return y * _sigmoid(y)

    ones = jnp.ones((LANES, LANES), BF16)

    def row_sumsq(x):
        return _dot((x * x).astype(BF16), ones)

    q, k, v, decay = [], [], [], []
    for h in heads:
        qh = conv_silu(0, h)
        kh = conv_silu(1, h)
        q.append(qh * lax.rsqrt(row_sumsq(qh) + EPS) * (LANES ** -0.5))
        k.append(kh * lax.rsqrt(row_sumsq(kh) + EPS))
        v.append(conv_silu(2, h))
        diff = jnp.where(causal, gc_all[:, h:h + 1] - gc_t[h:h + 1, :], 0.0)
        decay.append(jnp.where(causal, jnp.exp(diff), 0.0))
    beta = [beta_all[:, HEADS + h:HEADS + h + 1] for h in heads]
    eg = [eg_all[:, h:h + 1] for h in heads]

    qkk = [_dot_nt(jnp.concatenate([q[h], k[h]], axis=0).astype(BF16), k[h].astype(BF16)) for h in heads]
    qk = [(qkk[h][:c] * decay[h]).astype(BF16) for h in heads]
    a_low = [jnp.where(strict, beta[h] * qkk[h][c:] * decay[h], 0.0) for h in heads]

    pw = [jnp.where(diag, -a_low[h], 0.0) for h in heads]
    t = [eye + pw[h] for h in heads]
    for _ in range(TRI_BASE_LOG2 - 1):
        pw = [_bdot(pw[h], pw[h]) for h in heads]
        t = [t[h] + _bdot(t[h], pw[h]) for h in heads]
    for lvl in levels:
        y = [_bdot(jnp.where(lvl, a_low[h], 0.0), t[h]) for h in heads]
        t = [t[h] - _bdot(t[h], y[h]) for h in heads]

    rhs = [jnp.concatenate([v[h] * beta[h], k[h] * (beta[h] * eg[h])], axis=1) for h in heads]
    sol = [_bdot(t[h], rhs[h]) for h in heads]

    sb = [state_ref[h].astype(BF16) for h in heads]
    lhs1 = [jnp.concatenate([sol[h][:, LANES:], q[h] * eg[h]], axis=0).astype(BF16) for h in heads]
    r1 = [_dot(lhs1[h], sb[h]) for h in heads]
    vb = [(sol[h][:, :LANES] - r1[h][:c]).astype(BF16) for h in heads]
    lhs2 = [jnp.concatenate([qk[h], (k[h] * egl_all[:, h:h + 1]).T.astype(BF16)], axis=0) for h in heads]
    r2 = [_dot(lhs2[h], vb[h]) for h in heads]

    ng = ng_ref[...]
    for h in heads:
        state_ref[h] = state_ref[h] * gl_all[:, h:h + 1] + r2[h][c:]
        o = r1[h][c:] + r2[h][:c]
        gate = gate_ref[0, :, h * LANES:(h + 1) * LANES].astype(F32)
        o_n = o * lax.rsqrt(row_sumsq(o) * (1.0 / LANES) + EPS) * ng
        o_ref[0, :, h * LANES:(h + 1) * LANES] = (o_n * (gate * _sigmoid(gate))).astype(o_ref.dtype)

    halo_ref[...] = qkv_ref[0, c - HALO:, :].astype(F32)


def gdn_core(proj, ab, conv_w, a_log, dt_bias, norm_g, *, chunk):
    b, s, _ = proj.shape
    width = HEADS * LANES
    pad = lambda v: jnp.zeros((1, LANES), F32).at[0, :HEADS].set(v)
    return pl.pallas_call(
        _gdn_kernel,
        grid=(b, s // chunk),
        in_specs=[
            pl.BlockSpec((1, chunk, 3 * width), lambda bi, i: (bi, i, 0)),
            pl.BlockSpec((1, chunk, width), lambda bi, i: (bi, i, 3)),
            pl.BlockSpec((1, chunk, LANES), lambda bi, i: (bi, i, 0)),
            pl.BlockSpec((CONV_W, 3 * width), lambda bi, i: (0, 0)),
            pl.BlockSpec((1, LANES), lambda bi, i: (0, 0)),
            pl.BlockSpec((1, LANES), lambda bi, i: (0, 0)),
            pl.BlockSpec((1, LANES), lambda bi, i: (0, 0)),
        ],
        out_specs=pl.BlockSpec((1, chunk, width), lambda bi, i: (bi, i, 0)),
        out_shape=jax.ShapeDtypeStruct((b, s, width), BF16),
        scratch_shapes=[pltpu.VMEM((HEADS, LANES, LANES), F32), pltpu.VMEM((HALO, 3 * width), F32)],
        compiler_params=_cparams(("parallel", "arbitrary")),
        name="gdn_core",
    )(proj, proj, ab, conv_w, pad(a_log), pad(dt_bias), norm_g.reshape(1, LANES))


def _sb_blocks(qs, kbs, vbs, upper, carries, mask):
    n = range(len(qs))
    z = [_dot_nt(qs[i], kbs[i]) for i in n]
    log_sig, log_not = [], []
    for i in n:
        softplus = jnp.maximum(z[i], 0.0) + jnp.log(1.0 + jnp.exp(-jnp.abs(z[i])))
        log_sig.append(z[i] - softplus)
        log_not.append(-softplus if mask is None else jnp.where(mask, -softplus, 0.0))
    suffix = [_dot(log_not[i].astype(BF16), upper) for i in n]
    wgt = []
    for i in n:
        log_w = log_sig[i] + suffix[i] + carries[i]
        if mask is not None:
            log_w = jnp.where(mask, log_w, -jnp.inf)
        wgt.append(jnp.exp(log_w).astype(BF16))
    out = [_dot(wgt[i], vbs[i]) for i in n]
    total = [suffix[i][:, 0:1] + log_not[i][:, 0:1] for i in n]
    return out, total


SB_BLK = LANES
SB_STATIC_BANDS = 3
F32_EXP_UNDERFLOW = -104.0
NEG_BIG = -1e30


def _sb_kernel(q_ref, k_ref, v_ref, o_ref, acc_ref, carry_ref):
    gi = pl.program_id(2)
    n_sub = q_ref.shape[1] // SB_BLK
    first_blk = gi * n_sub
    row = lax.broadcasted_iota(jnp.int32, (SB_BLK, SB_BLK), 0)
    col = lax.broadcasted_iota(jnp.int32, (SB_BLK, SB_BLK), 1)
    upper = (row > col).astype(BF16)

    def rows(i):
        return slice(i * SB_BLK, (i + 1) * SB_BLK)

    def band(d, first):
        subs = range(n_sub)
        blk = [first_blk + i - d for i in subs]
        start = [pl.multiple_of(jnp.maximum(blk[i], 0) * SB_BLK, SB_BLK) for i in subs]
        qs = [q_ref[0, rows(i), :] for i in subs]
        kbs = [k_ref[0, pl.ds(start[i], SB_BLK), :] for i in subs]
        vbs = [v_ref[0, pl.ds(start[i], SB_BLK), :] for i in subs]
        if first:
            out, total = _sb_blocks(qs, kbs, vbs, upper, [0.0] * n_sub, col < row)
            for i in subs:
                acc_ref[rows(i), :] = out[i]
                carry_ref[:, i:i + 1] = total[i]
        else:
            carries = [jnp.where(blk[i] >= 0, carry_ref[:, i:i + 1], NEG_BIG) for i in subs]
            out, total = _sb_blocks(qs, kbs, vbs, upper, carries, None)
            for i in subs:
                acc_ref[rows(i), :] += out[i]
                carry_ref[:, i:i + 1] = carries[i] + total[i]

    band(0, True)
    for d in range(1, SB_STATIC_BANDS):
        band(d, False)

    def any_row_live():
        return jnp.max(carry_ref[:, 0:n_sub]) >= F32_EXP_UNDERFLOW

    def unfinished(state):
        d, live = state
        return jnp.logical_and(d <= first_blk + n_sub - 1, live)

    def body(state):
        d, _ = state
        band(d, False)
        return d + 1, any_row_live()

    lax.while_loop(unfinished, body, (jnp.int32(SB_STATIC_BANDS), any_row_live()))
    o_ref[0] = acc_ref[...].astype(o_ref.dtype)


def sb_attention(q, kv, *, tq):
    b, s, width = q.shape
    return pl.pallas_call(
        _sb_kernel,
        grid=(b, HEADS, s // tq),
        in_specs=[
            pl.BlockSpec((1, tq, LANES), lambda bi, h, i: (bi, i, h)),
            pl.BlockSpec((1, s, LANES), lambda bi, h, i: (bi, 0, h)),
            pl.BlockSpec((1, s, LANES), lambda bi, h, i: (bi, 0, HEADS + h)),
        ],
        out_specs=pl.BlockSpec((1, tq, LANES), lambda bi, h, i: (bi, i, h)),
        out_shape=jax.ShapeDtypeStruct((b, s, width), BF16),
        scratch_shapes=[pltpu.VMEM((tq, LANES), F32), pltpu.VMEM((SB_BLK, LANES), F32)],
        compiler_params=_cparams(("parallel", "parallel", "arbitrary")),
        name="sb_attention",
    )(q, kv, kv)


def kernel(x, p, ln_mix, ln_ffn, ln_ple, gdn_w_in, gdn_conv, gdn_a_log, gdn_dt_bias, gdn_norm,
           gdn_w_out, kv_norm, w_kv, k_norm, sb_w_q, sb_q_norm, sb_w_out, ffn_w_in, ffn_w_out,
           ple_w_proj, ple_w_gate):
    b, s, d = x.shape
    depth = p.shape[0]
    n_a = gdn_w_in.shape[0]
    m = b * s
    width = HEADS * LANES

    h = x.reshape(m, d)
    kv = None
    for i in range(depth):
        if i < n_a:
            w_in = gdn_w_in[i]
            w_main = w_in[:, :4 * width].astype(BF16)
            n_ab = w_in.shape[1] - 4 * width
            w_ab = jnp.pad(w_in[:, 4 * width:], ((0, 0), (0, LANES - n_ab))).astype(BF16)
            proj_main, proj_ab = proj(h, ln_mix[i], w_main, tm=TM, w_extra=w_ab)
            o = gdn_core(proj_main.reshape(b, s, 4 * width), proj_ab.reshape(b, s, LANES), gdn_conv[i],
                         gdn_a_log[i], gdn_dt_bias[i], gdn_norm[i], chunk=LANES)
            w_out = gdn_w_out[i]
        else:
            j = i - n_a
            q = proj(h, ln_mix[i], sb_w_q[j].astype(BF16), tm=TM, head_g=sb_q_norm[j], n_normed=width,
                     out_scale=LANES ** -0.5)
            o = sb_attention(q.reshape(b, s, width), kv, tq=8 * SB_BLK)
            w_out = sb_w_out[j]
        w1, w2 = _ffn_weights(ffn_w_in[i], ffn_w_out[i], FFN_CHUNK)
        h = post_mixer(h, o.reshape(m, width), p[i].reshape(m, -1), w_out.astype(BF16), ln_ffn[i], w1, w2,
                       ln_ple[i], ple_w_proj[i].astype(BF16), ple_w_gate[i].astype(BF16), tm=TM)
        if i == n_a - 1:
            kv = proj(h, kv_norm, w_kv.astype(BF16), tm=TM, head_g=k_norm,
                      n_normed=width).reshape(b, s, 2 * width)
    return h.reshape(b, s, d)
```

```python
import functools

import jax
import jax.numpy as jnp
from jax import lax
from jax.experimental import pallas as pl
from jax.experimental.pallas import tpu as pltpu

EPS = 1e-6
LANES = 128
HEADS = 8
CONV_W = 4
HALO = 16
VMEM_LIMIT = 56 * 1024 * 1024
TM = 512
FFN_CHUNK = 256

F32 = jnp.float32
BF16 = jnp.bfloat16
HI = lax.Precision.HIGHEST


def _cparams(sem):
    return pltpu.CompilerParams(dimension_semantics=sem, vmem_limit_bytes=VMEM_LIMIT)


def _dot(a, b):
    return jnp.dot(a, b, preferred_element_type=F32)


def _dot_nt(a, b):
    return lax.dot_general(a, b, (((1,), (1,)), ((), ())), preferred_element_type=F32)


def _dot_hi(a, b):
    return jnp.dot(a, b, preferred_element_type=F32, precision=HI)


def _rms(x, g):
    return x * lax.rsqrt(jnp.mean(x * x, axis=-1, keepdims=True) + EPS) * g


def _sigmoid(x):
    return 1.0 / (1.0 + jnp.exp(-x))


def _resident(arr, layer=None, cols=None):
    shape = arr.shape if layer is None else arr.shape[1:]
    if cols is not None:
        shape = shape[:-1] + (cols,)
    zeros = (0,) * len(shape)
    if layer is None:
        return pl.BlockSpec(shape, lambda i: zeros, pipeline_mode=pl.Buffered(1))
    return pl.BlockSpec((None,) + shape, lambda i: (layer,) + zeros, pipeline_mode=pl.Buffered(1))


def _proj_kernel(x_ref, g_ref, w_ref, hg_ref, o_ref, *, n_normed, out_scale):
    xn = _rms(x_ref[...], g_ref[...]).astype(BF16)
    hg = hg_ref[...]
    n = w_ref.shape[1]
    step = 4 * LANES
    for lo in range(0, n, step):
        y = _dot(xn, w_ref[:, lo:lo + step])
        for c in range(step // LANES):
            col = lo + c * LANES
            yc = y[:, c * LANES:(c + 1) * LANES]
            if col < n_normed:
                yc = _rms(yc, hg) * out_scale
            o_ref[:, col:col + LANES] = yc.astype(o_ref.dtype)


def proj(x, g, w, head_g, *, tm, g_layer=None, w_layer=None, n_normed=0, out_scale=1.0):
    m, d = x.shape
    n = w.shape[-1]
    kern = functools.partial(_proj_kernel, n_normed=n_normed, out_scale=out_scale)
    return pl.pallas_call(
        kern,
        grid=(m // tm,),
        in_specs=[pl.BlockSpec((tm, d), lambda i: (i, 0)), _resident(g, g_layer), _resident(w, w_layer),
                  _resident(head_g, w_layer)],
        out_specs=pl.BlockSpec((tm, n), lambda i: (i, 0)),
        out_shape=jax.ShapeDtypeStruct((m, n), BF16),
        compiler_params=_cparams(("parallel",)),
        name="proj",
    )(x, g, w, head_g)


def _proj_gdn_kernel(x_ref, xh_ref, g_ref, w_ref, wab_ref, conv_ref, o_ref, ab_ref, *, tiles_per_seq):
    i = pl.program_id(0)
    tm = x_ref.shape[0]
    width = HEADS * LANES
    g = g_ref[...]
    xn = _rms(x_ref[...], g).astype(BF16)
    xa = jnp.concatenate([_rms(xh_ref[...], g).astype(BF16), xn], axis=0)
    rows = lax.broadcasted_iota(jnp.int32, (HALO + tm, 1), 0)
    keep = jnp.logical_or(rows >= HALO, i % tiles_per_seq != 0)
    ones = jnp.ones((LANES, LANES), BF16)
    step = 4 * LANES
    chunks = list(range(0, 3 * width, step))
    y_next = _dot(xa, w_ref[:, 0:step])
    for lo in chunks:
        y = jnp.where(keep, y_next, 0.0)
        if lo + step < 3 * width:
            y_next = _dot(xa, w_ref[:, lo + step:lo + 2 * step])
        tap = lambda k: conv_ref[CONV_W - 1 - k:CONV_W - k, lo:lo + step]
        y1 = pltpu.roll(y, 1, 0)
        acc = (tap(0) * y + tap(1) * y1 + pltpu.roll(tap(2) * y + tap(3) * y1, 2, 0))[HALO:]
        act = acc * _sigmoid(acc)
        for c in range(step // LANES):
            col = lo + c * LANES
            a = act[:, c * LANES:(c + 1) * LANES]
            if col < 2 * width:
                a = a * lax.rsqrt(_dot((a * a).astype(BF16), ones) + EPS)
                if col < width:
                    a = a * (LANES ** -0.5)
            o_ref[:, col:col + LANES] = a.astype(o_ref.dtype)
    for lo in range(3 * width, 4 * width, step):
        y = _dot(xn, w_ref[:, lo:lo + step])
        o_ref[:, lo:lo + step] = (y * _sigmoid(y)).astype(o_ref.dtype)
    ab_ref[...] = _dot(xn, wab_ref[...])


def proj_gdn(x, g, w_in, w_ab, conv_w, *, tm, layer, seq_len):
    m, d = x.shape
    n = 4 * HEADS * LANES
    halo_blocks = tm // HALO
    kern = functools.partial(_proj_gdn_kernel, tiles_per_seq=seq_len // tm)
    return pl.pallas_call(
        kern,
        grid=(m // tm,),
        in_specs=[
            pl.BlockSpec((tm, d), lambda i: (i, 0)),
            pl.BlockSpec((HALO, d), lambda i: (jnp.maximum(i * halo_blocks - 1, 0), 0)),
            _resident(g, layer), _resident(w_in, layer, cols=n), _resident(w_ab, layer),
            _resident(conv_w, layer),
        ],
        out_specs=[pl.BlockSpec((tm, n), lambda i: (i, 0)), pl.BlockSpec((tm, LANES), lambda i: (i, 0))],
        out_shape=[jax.ShapeDtypeStruct((m, n), BF16), jax.ShapeDtypeStruct((m, LANES), F32)],
        compiler_params=_cparams(("parallel",)),
        name="proj_gdn",
    )(x, x, g, w_in, w_ab, conv_w)


def _post_kernel(h_ref, o_ref, p_ref, wo_ref, gf_ref, w1_ref, w2_ref, gp_ref, wp_ref, wg_ref, out_ref,
                 hn_ref, acc_ref):
    hidden = w2_ref.shape[0]
    h1 = h_ref[...] + _dot(o_ref[...], wo_ref[...])
    hn_ref[...] = _rms(h1, gf_ref[...]).astype(BF16)
    acc_ref[...] = h1
    def gate_and_linear(lo):
        hn = hn_ref[...]
        return (_dot(hn, w1_ref[:, lo:lo + FFN_CHUNK]),
                _dot(hn, w1_ref[:, hidden + lo:hidden + lo + FFN_CHUNK]))

    a, b = gate_and_linear(0)
    for lo in range(0, hidden, FFN_CHUNK):
        act = (a * _sigmoid(a) * b).astype(BF16)
        if lo + FFN_CHUNK < hidden:
            a, b = gate_and_linear(lo + FFN_CHUNK)
        acc_ref[...] += _dot(act, w2_ref[lo:lo + FFN_CHUNK, :])
    h2 = acc_ref[...]
    gate = _sigmoid(_dot(_rms(h2, gp_ref[...]).astype(BF16), wg_ref[...]))
    emb = _dot(p_ref[...].astype(BF16), wp_ref[...])
    out_ref[...] = h2 + emb * gate


def post_mixer(h, o, p, w_out, g_ffn, w1, w2, g_ple, wp, wg, *, tm, layer, mixer_layer):
    m, d = h.shape
    return pl.pallas_call(
        _post_kernel,
        grid=(m // tm,),
        in_specs=[
            pl.BlockSpec((tm, d), lambda i: (i, 0)),
            pl.BlockSpec((tm, o.shape[1]), lambda i: (i, 0)),
            pl.BlockSpec((None, tm, p.shape[2]), lambda i: (layer, i, 0)),
            _resident(w_out, mixer_layer), _resident(g_ffn, layer), _resident(w1, layer), _resident(w2, layer),
            _resident(g_ple, layer), _resident(wp, layer), _resident(wg, layer),
        ],
        out_specs=pl.BlockSpec((tm, d), lambda i: (i, 0)),
        out_shape=jax.ShapeDtypeStruct((m, d), F32),
        scratch_shapes=[pltpu.VMEM((tm, d), BF16), pltpu.VMEM((tm, d), F32)],
        compiler_params=_cparams(("parallel",)),
        name="post_mixer",
    )(h, o, p, w_out, g_ffn, w1, w2, g_ple, wp, wg)


TRI_BASE_LOG2 = 4


def _tri_masks(row, col):
    c = row.shape[0]
    eye = (row == col).astype(F32)
    diag = (row >> TRI_BASE_LOG2) == (col >> TRI_BASE_LOG2)
    levels = []
    sh = TRI_BASE_LOG2
    while (1 << sh) < c:
        rb = row >> sh
        levels.append(((rb & 1) == 1) & ((col >> sh) == rb - 1))
        sh += 1
    return eye, diag, levels


def _bdot(a, b):
    return _dot(a.astype(BF16), b.astype(BF16))


def _split_bf16(x):
    hi = x.astype(BF16)
    return hi, (x - hi.astype(F32)).astype(BF16)


def _gdn_kernel(qkv_ref, gate_ref, ab_ref, alog_ref, dtb_ref, ng_ref, o_ref, state_ref):
    i = pl.program_id(1)
    c = qkv_ref.shape[1]
    width = HEADS * LANES
    heads = range(HEADS)

    @pl.when(i == 0)
    def _():
        state_ref[...] = jnp.zeros_like(state_ref)

    row = lax.broadcasted_iota(jnp.int32, (c, c), 0)
    col = lax.broadcasted_iota(jnp.int32, (c, c), 1)
    causal = row >= col
    strict = row > col
    eye, diag, levels = _tri_masks(row, col)

    ab = ab_ref[0]
    x_dt = ab + dtb_ref[...]
    softplus = jnp.maximum(x_dt, 0.0) + jnp.log(1.0 + jnp.exp(-jnp.abs(x_dt)))
    g_all = -jnp.exp(alog_ref[...]) * softplus
    beta_all = _sigmoid(ab)
    g_hi, g_lo = _split_bf16(g_all)
    tri_incl = causal.astype(BF16)
    gc_all = _dot(tri_incl, g_hi) + _dot(tri_incl, g_lo)
    gc_t = gc_all.T
    g_last = gc_all[c - 1:c, :]
    eg_all = jnp.exp(gc_all)
    egl_all = jnp.exp(g_last - gc_all)
    gl_all = jnp.exp(g_last)

    ones = jnp.ones((LANES, LANES), BF16)

    def section(sec, h):
        lo = sec * width + h * LANES
        return qkv_ref[0, :, lo:lo + LANES]

    qb = [section(0, h) for h in heads]
    kb = [section(1, h) for h in heads]
    q = [qb[h].astype(F32) for h in heads]
    k = [kb[h].astype(F32) for h in heads]
    v = [section(2, h).astype(F32) for h in heads]
    decay = []
    for h in heads:
        diff = jnp.where(causal, gc_all[:, h:h + 1] - gc_t[h:h + 1, :], 0.0)
        decay.append(jnp.where(causal, jnp.exp(diff), 0.0))
    beta = [beta_all[:, HEADS + h:HEADS + h + 1] for h in heads]
    eg = [eg_all[:, h:h + 1] for h in heads]

    qkk = [_dot_nt(jnp.concatenate([qb[h], kb[h]], axis=0), kb[h]) for h in heads]
    qk = [(qkk[h][:c] * decay[h]).astype(BF16) for h in heads]
    a_low = [jnp.where(strict, beta[h] * qkk[h][c:] * decay[h], 0.0) for h in heads]

    pw = [jnp.where(diag, -a_low[h], 0.0) for h in heads]
    t = [eye + pw[h] for h in heads]
    for _ in range(TRI_BASE_LOG2 - 1):
        pw = [_bdot(pw[h], pw[h]) for h in heads]
        t = [t[h] + _bdot(t[h], pw[h]) for h in heads]
    for lvl in levels:
        y = [_bdot(jnp.where(lvl, a_low[h], 0.0), t[h]) for h in heads]
        t = [t[h] - _bdot(t[h], y[h]) for h in heads]

    rhs = [jnp.concatenate([v[h] * beta[h], k[h] * (beta[h] * eg[h])], axis=1) for h in heads]
    sol = [_bdot(t[h], rhs[h]) for h in heads]

    sb = [state_ref[h].astype(BF16) for h in heads]
    lhs1 = [jnp.concatenate([sol[h][:, LANES:], q[h] * eg[h]], axis=0).astype(BF16) for h in heads]
    r1 = [_dot(lhs1[h], sb[h]) for h in heads]
    vb = [(sol[h][:, :LANES] - r1[h][:c]).astype(BF16) for h in heads]
    lhs2 = [jnp.concatenate([qk[h], (k[h] * egl_all[:, h:h + 1]).T.astype(BF16)], axis=0) for h in heads]
    r2 = [_dot(lhs2[h], vb[h]) for h in heads]

    ng = ng_ref[...]
    for h in heads:
        state_ref[h] = state_ref[h] * gl_all[:, h:h + 1] + r2[h][c:]
        o = r1[h][c:] + r2[h][:c]
        gate = gate_ref[0, :, h * LANES:(h + 1) * LANES].astype(F32)
        mean_sq = _dot((o * o).astype(BF16), ones) * (1.0 / LANES)
        o_ref[0, :, h * LANES:(h + 1) * LANES] = (o * lax.rsqrt(mean_sq + EPS) * ng * gate).astype(o_ref.dtype)


def gdn_core(proj, ab, a_log, dt_bias, norm_g, *, chunk):
    b, s, _ = proj.shape
    width = HEADS * LANES
    pad = lambda v: jnp.zeros((1, LANES), F32).at[0, :HEADS].set(v)
    return pl.pallas_call(
        _gdn_kernel,
        grid=(b, s // chunk),
        in_specs=[
            pl.BlockSpec((1, chunk, 3 * width), lambda bi, i: (bi, i, 0)),
            pl.BlockSpec((1, chunk, width), lambda bi, i: (bi, i, 3)),
            pl.BlockSpec((1, chunk, LANES), lambda bi, i: (bi, i, 0)),
            pl.BlockSpec((1, LANES), lambda bi, i: (0, 0)),
            pl.BlockSpec((1, LANES), lambda bi, i: (0, 0)),
            pl.BlockSpec((1, LANES), lambda bi, i: (0, 0)),
        ],
        out_specs=pl.BlockSpec((1, chunk, width), lambda bi, i: (bi, i, 0)),
        out_shape=jax.ShapeDtypeStruct((b, s, width), BF16),
        scratch_shapes=[pltpu.VMEM((HEADS, LANES, LANES), F32)],
        compiler_params=_cparams(("parallel", "arbitrary")),
        name="gdn_core",
    )(proj, proj, ab, pad(a_log), pad(dt_bias), norm_g.reshape(1, LANES))


def _sb_bands(qs, kbs, vbs, upper, carries, valid, diag_mask):
    bands, subs = range(len(kbs)), range(len(qs))
    z = [[_dot_nt(qs[i], kbs[d][i]) for i in subs] for d in bands]
    log_sig, log_not = [], []
    for d in bands:
        ls, ln = [], []
        for i in subs:
            nz = -z[d][i]
            nsp = jnp.minimum(nz, 0.0) - jnp.log(1.0 + jnp.exp(jnp.minimum(z[d][i], nz)))
            ls.append(z[d][i] + nsp)
            ln.append(jnp.where(diag_mask, nsp, 0.0) if (d == 0 and diag_mask is not None) else nsp)
        log_sig.append(ls)
        log_not.append(ln)
    suffix = [[_dot(log_not[d][i].astype(BF16), upper) for i in subs] for d in bands]
    wgt = [[] for _ in subs]
    carry_out = []
    for i in subs:
        carry = carries[i]
        for d in bands:
            if valid[d][i] is not None:
                carry = jnp.where(valid[d][i], carry, NEG_BIG)
            log_w = log_sig[d][i] + suffix[d][i] + carry
            if d == 0 and diag_mask is not None:
                log_w = jnp.where(diag_mask, log_w, -jnp.inf)
            wgt[i].append(jnp.exp(log_w).astype(BF16))
            carry = carry + (suffix[d][i][:, 0:1] + log_not[d][i][:, 0:1])
        carry_out.append(carry)
    out = [_dot(jnp.concatenate(wgt[i], axis=1), jnp.concatenate([vbs[d][i] for d in bands], axis=0))
           for i in subs]
    return out, carry_out


SB_BLK = LANES
SB_STATIC_BANDS = 3
F32_EXP_UNDERFLOW = -104.0
NEG_BIG = -1e30


def _sb_kernel(q_ref, k_ref, v_ref, o_ref, acc_ref, carry_ref):
    gi = pl.program_id(2)
    n_sub = q_ref.shape[1] // SB_BLK
    first_blk = gi * n_sub
    row = lax.broadcasted_iota(jnp.int32, (SB_BLK, SB_BLK), 0)
    col = lax.broadcasted_iota(jnp.int32, (SB_BLK, SB_BLK), 1)
    upper = (row > col).astype(BF16)

    def rows(i):
        return slice(i * SB_BLK, (i + 1) * SB_BLK)

    subs = range(n_sub)

    def bands(d0, n_bands, first):
        qs = [q_ref[0, rows(i), :] for i in subs]
        kbs, vbs, valid = [], [], []
        for d in range(n_bands):
            blk = [first_blk + i - (d0 + d) for i in subs]
            start = [pl.multiple_of(jnp.maximum(blk[i], 0) * SB_BLK, SB_BLK) for i in subs]
            kbs.append([k_ref[0, pl.ds(start[i], SB_BLK), :] for i in subs])
            vbs.append([v_ref[0, pl.ds(start[i], SB_BLK), :] for i in subs])
            valid.append([None if (first and d == 0) else blk[i] >= 0 for i in subs])
        carries = [0.0 if first else carry_ref[:, i:i + 1] for i in subs]
        out, carry_out = _sb_bands(qs, kbs, vbs, upper, carries, valid, col < row if first else None)
        for i in subs:
            if first:
                acc_ref[rows(i), :] = out[i]
            else:
                acc_ref[rows(i), :] += out[i]
            carry_ref[:, i:i + 1] = carry_out[i]

    bands(0, SB_STATIC_BANDS, True)

    def any_row_live():
        return jnp.max(carry_ref[:, 0:n_sub]) >= F32_EXP_UNDERFLOW

    def unfinished(state):
        d, live = state
        return jnp.logical_and(d <= first_blk + n_sub - 1, live)

    def body(state):
        d, _ = state
        bands(d, 1, False)
        return d + 1, any_row_live()

    lax.while_loop(unfinished, body, (jnp.int32(SB_STATIC_BANDS), any_row_live()))
    o_ref[0] = acc_ref[...].astype(o_ref.dtype)


def sb_attention(q, kv, *, tq):
    b, s, width = q.shape
    return pl.pallas_call(
        _sb_kernel,
        grid=(b, HEADS, s // tq),
        in_specs=[
            pl.BlockSpec((1, tq, LANES), lambda bi, h, i: (bi, i, h)),
            pl.BlockSpec((1, s, LANES), lambda bi, h, i: (bi, 0, h)),
            pl.BlockSpec((1, s, LANES), lambda bi, h, i: (bi, 0, HEADS + h)),
        ],
        out_specs=pl.BlockSpec((1, tq, LANES), lambda bi, h, i: (bi, i, h)),
        out_shape=jax.ShapeDtypeStruct((b, s, width), BF16),
        scratch_shapes=[pltpu.VMEM((tq, LANES), F32), pltpu.VMEM((SB_BLK, LANES), F32)],
        compiler_params=_cparams(("parallel", "parallel", "arbitrary")),
        name="sb_attention",
    )(q, kv, kv)


def kernel(x, p, ln_mix, ln_ffn, ln_ple, gdn_w_in, gdn_conv, gdn_a_log, gdn_dt_bias, gdn_norm,
           gdn_w_out, kv_norm, w_kv, k_norm, sb_w_q, sb_q_norm, sb_w_out, ffn_w_in, ffn_w_out,
           ple_w_proj, ple_w_gate):
    b, s, d = x.shape
    depth = p.shape[0]
    n_a = gdn_w_in.shape[0]
    m = b * s
    width = HEADS * LANES

    bf = lambda w: w.astype(BF16)
    gains = lambda g: g.reshape(g.shape[0], 1, g.shape[-1])
    gdn_w_in_b, gdn_w_out_b, sb_w_q_b, sb_w_out_b = bf(gdn_w_in), bf(gdn_w_out), bf(sb_w_q), bf(sb_w_out)
    ffn_w_in_b, ffn_w_out_b, ple_w_proj_b, ple_w_gate_b = bf(ffn_w_in), bf(ffn_w_out), bf(ple_w_proj), bf(ple_w_gate)
    n_ab = gdn_w_in.shape[2] - 4 * width
    gdn_w_ab_b = bf(jnp.pad(gdn_w_in[:, :, 4 * width:], ((0, 0), (0, 0), (0, LANES - n_ab))))
    ln_mix_g, ln_ffn_g, ln_ple_g, sb_q_norm_g = gains(ln_mix), gains(ln_ffn), gains(ln_ple), gains(sb_q_norm)
    p_rows = p.reshape(depth, m, p.shape[-1])

    h = x.reshape(m, d)
    kv = None
    for i in range(depth):
        if i < n_a:
            proj_main, proj_ab = proj_gdn(h, ln_mix_g, gdn_w_in_b, gdn_w_ab_b, gdn_conv, tm=TM, layer=i, seq_len=s)
            o = gdn_core(proj_main.reshape(b, s, 4 * width), proj_ab.reshape(b, s, LANES),
                         gdn_a_log[i], gdn_dt_bias[i], gdn_norm[i], chunk=LANES)
            w_out, mixer_layer = gdn_w_out_b, i
        else:
            j = i - n_a
            q = proj(h, ln_mix_g, sb_w_q_b, sb_q_norm_g, tm=TM, g_layer=i, w_layer=j, n_normed=width,
                     out_scale=LANES ** -0.5)
            o = sb_attention(q.reshape(b, s, width), kv, tq=8 * SB_BLK)
            w_out, mixer_layer = sb_w_out_b, j
        h = post_mixer(h, o.reshape(m, width), p_rows, w_out, ln_ffn_g, ffn_w_in_b, ffn_w_out_b, ln_ple_g,
                       ple_w_proj_b, ple_w_gate_b, tm=TM, layer=i, mixer_layer=mixer_layer)
        if i == n_a - 1:
            kv = proj(h, kv_norm.reshape(1, d), bf(w_kv), k_norm.reshape(1, LANES), tm=TM,
                      n_normed=width).reshape(b, s, 2 * width)
    return h.reshape(b, s, d)
```

```python
import functools

import jax
import jax.numpy as jnp
from jax import lax
from jax.experimental import pallas as pl
from jax.experimental.pallas import tpu as pltpu

EPS = 1e-6
LANES = 128
HEADS = 8
CONV_W = 4
HALO = 16
VMEM_LIMIT = 56 * 1024 * 1024
TM = 512
FFN_CHUNK = 256

F32 = jnp.float32
BF16 = jnp.bfloat16
HI = lax.Precision.HIGHEST


def _cparams(sem):
    return pltpu.CompilerParams(dimension_semantics=sem, vmem_limit_bytes=VMEM_LIMIT)


def _dot(a, b):
    return jnp.dot(a, b, preferred_element_type=F32)


def _dot_nt(a, b):
    return lax.dot_general(a, b, (((1,), (1,)), ((), ())), preferred_element_type=F32)


def _dot_hi(a, b):
    return jnp.dot(a, b, preferred_element_type=F32, precision=HI)


def _rms(x, g):
    return x * lax.rsqrt(jnp.mean(x * x, axis=-1, keepdims=True) + EPS) * g


def _sigmoid(x):
    return 1.0 / (1.0 + jnp.exp(-x))


def _resident(arr, layer=None, cols=None):
    shape = arr.shape if layer is None else arr.shape[1:]
    if cols is not None:
        shape = shape[:-1] + (cols,)
    zeros = (0,) * len(shape)
    if layer is None:
        return pl.BlockSpec(shape, lambda i: zeros, pipeline_mode=pl.Buffered(1))
    return pl.BlockSpec((None,) + shape, lambda i: (layer,) + zeros, pipeline_mode=pl.Buffered(1))


def _proj_kernel(x_ref, g_ref, w_ref, hg_ref, o_ref, *, n_normed, out_scale):
    xn = _rms(x_ref[...], g_ref[...]).astype(BF16)
    hg = hg_ref[...]
    n = w_ref.shape[1]
    step = 2 * LANES
    y_next = _dot(xn, w_ref[:, 0:step])
    for lo in range(0, n, step):
        y = y_next
        if lo + step < n:
            y_next = _dot(xn, w_ref[:, lo + step:lo + 2 * step])
        for c in range(step // LANES):
            col = lo + c * LANES
            yc = y[:, c * LANES:(c + 1) * LANES]
            if col < n_normed:
                yc = _rms(yc, hg) * out_scale
            o_ref[:, col:col + LANES] = yc.astype(o_ref.dtype)


def proj(x, g, w, head_g, *, tm, g_layer=None, w_layer=None, n_normed=0, out_scale=1.0):
    m, d = x.shape
    n = w.shape[-1]
    kern = functools.partial(_proj_kernel, n_normed=n_normed, out_scale=out_scale)
    return pl.pallas_call(
        kern,
        grid=(m // tm,),
        in_specs=[pl.BlockSpec((tm, d), lambda i: (i, 0)), _resident(g, g_layer), _resident(w, w_layer),
                  _resident(head_g, w_layer)],
        out_specs=pl.BlockSpec((tm, n), lambda i: (i, 0)),
        out_shape=jax.ShapeDtypeStruct((m, n), BF16),
        compiler_params=_cparams(("parallel",)),
        name="proj",
    )(x, g, w, head_g)


def _proj_gdn_kernel(x_ref, xh_ref, g_ref, w_ref, wab_ref, conv_ref, o_ref, ab_ref, *, tiles_per_seq):
    i = pl.program_id(0)
    tm = x_ref.shape[0]
    width = HEADS * LANES
    g = g_ref[...]
    xn = _rms(x_ref[...], g).astype(BF16)
    xa = jnp.concatenate([_rms(xh_ref[...], g).astype(BF16), xn], axis=0)
    seq_start = i % tiles_per_seq == 0
    ones = jnp.ones((LANES, LANES), BF16)
    step = 4 * LANES
    chunks = list(range(0, 3 * width, step))
    y_next = _dot(xa, w_ref[:, 0:step])
    for lo in chunks:
        y = jnp.concatenate([jnp.where(seq_start, 0.0, y_next[:HALO]), y_next[HALO:]], axis=0)
        if lo + step < 3 * width:
            y_next = _dot(xa, w_ref[:, lo + step:lo + 2 * step])
        tap = lambda k: conv_ref[CONV_W - 1 - k:CONV_W - k, lo:lo + step]
        y1 = pltpu.roll(y, 1, 0)
        acc = (tap(0) * y + tap(1) * y1 + pltpu.roll(tap(2) * y + tap(3) * y1, 2, 0))[HALO:]
        act = acc * _sigmoid(acc)
        for c in range(step // LANES):
            col = lo + c * LANES
            a = act[:, c * LANES:(c + 1) * LANES]
            if col < 2 * width:
                a = a * lax.rsqrt(_dot((a * a).astype(BF16), ones) + EPS)
                if col < width:
                    a = a * (LANES ** -0.5)
            o_ref[:, col:col + LANES] = a.astype(o_ref.dtype)
    for lo in range(3 * width, 4 * width, step):
        y = _dot(xn, w_ref[:, lo:lo + step])
        o_ref[:, lo:lo + step] = (y * _sigmoid(y)).astype(o_ref.dtype)
    ab_ref[...] = _dot(xn, wab_ref[...])


def proj_gdn(x, g, w_in, w_ab, conv_w, *, tm, layer, seq_len):
    m, d = x.shape
    n = 4 * HEADS * LANES
    halo_blocks = tm // HALO
    kern = functools.partial(_proj_gdn_kernel, tiles_per_seq=seq_len // tm)
    return pl.pallas_call(
        kern,
        grid=(m // tm,),
        in_specs=[
            pl.BlockSpec((tm, d), lambda i: (i, 0)),
            pl.BlockSpec((HALO, d), lambda i: (jnp.maximum(i * halo_blocks - 1, 0), 0)),
            _resident(g, layer), _resident(w_in, layer, cols=n), _resident(w_ab, layer),
            _resident(conv_w, layer),
        ],
        out_specs=[pl.BlockSpec((tm, n), lambda i: (i, 0)), pl.BlockSpec((tm, LANES), lambda i: (i, 0))],
        out_shape=[jax.ShapeDtypeStruct((m, n), BF16), jax.ShapeDtypeStruct((m, LANES), F32)],
        compiler_params=_cparams(("parallel",)),
        name="proj_gdn",
    )(x, x, g, w_in, w_ab, conv_w)


def _post_kernel(h_ref, o_ref, p_ref, wo_ref, gf_ref, w1_ref, w2_ref, gp_ref, wp_ref, wg_ref, out_ref,
                 hn_ref, acc_ref):
    hidden = w2_ref.shape[0]
    h1 = h_ref[...] + _dot(o_ref[...], wo_ref[...])
    hn_ref[...] = _rms(h1, gf_ref[...]).astype(BF16)
    acc_ref[...] = h1
    def gate_and_linear(lo):
        hn = hn_ref[...]
        return (_dot(hn, w1_ref[:, lo:lo + FFN_CHUNK]),
                _dot(hn, w1_ref[:, hidden + lo:hidden + lo + FFN_CHUNK]))

    a, b = gate_and_linear(0)
    for lo in range(0, hidden, FFN_CHUNK):
        act = (a * _sigmoid(a) * b).astype(BF16)
        if lo + FFN_CHUNK < hidden:
            a, b = gate_and_linear(lo + FFN_CHUNK)
        acc_ref[...] += _dot(act, w2_ref[lo:lo + FFN_CHUNK, :])
    h2 = acc_ref[...]
    gate = _sigmoid(_dot(_rms(h2, gp_ref[...]).astype(BF16), wg_ref[...]))
    emb = _dot(p_ref[...].astype(BF16), wp_ref[...])
    out_ref[...] = h2 + emb * gate


def post_mixer(h, o, p, w_out, g_ffn, w1, w2, g_ple, wp, wg, *, tm, layer, mixer_layer):
    m, d = h.shape
    return pl.pallas_call(
        _post_kernel,
        grid=(m // tm,),
        in_specs=[
            pl.BlockSpec((tm, d), lambda i: (i, 0)),
            pl.BlockSpec((tm, o.shape[1]), lambda i: (i, 0)),
            pl.BlockSpec((None, tm, p.shape[2]), lambda i: (layer, i, 0)),
            _resident(w_out, mixer_layer), _resident(g_ffn, layer), _resident(w1, layer), _resident(w2, layer),
            _resident(g_ple, layer), _resident(wp, layer), _resident(wg, layer),
        ],
        out_specs=pl.BlockSpec((tm, d), lambda i: (i, 0)),
        out_shape=jax.ShapeDtypeStruct((m, d), F32),
        scratch_shapes=[pltpu.VMEM((tm, d), BF16), pltpu.VMEM((tm, d), F32)],
        compiler_params=_cparams(("parallel",)),
        name="post_mixer",
    )(h, o, p, w_out, g_ffn, w1, w2, g_ple, wp, wg)


TRI_BASE_LOG2 = 4
GDN_CHUNK = LANES
GDN_CHUNKS_PER_STEP = 4


def _tri_masks(row, col):
    c = row.shape[0]
    eye = (row == col).astype(F32)
    diag = (row >> TRI_BASE_LOG2) == (col >> TRI_BASE_LOG2)
    levels = []
    sh = TRI_BASE_LOG2
    while (1 << sh) < c:
        rb = row >> sh
        levels.append(((rb & 1) == 1) & ((col >> sh) == rb - 1))
        sh += 1
    return eye, diag, levels


def _bdot(a, b):
    return _dot(a.astype(BF16), b.astype(BF16))


def _split_bf16(x):
    hi = x.astype(BF16)
    return hi, (x - hi.astype(F32)).astype(BF16)


def _gdn_kernel(qkv_ref, gate_ref, ab_ref, alog_ref, dtb_ref, ng_ref, o_ref, state_ref):
    i = pl.program_id(1)
    c = GDN_CHUNK
    n_ck = qkv_ref.shape[1] // c
    width = HEADS * LANES
    heads = range(HEADS)
    items = [(ck, h) for ck in range(n_ck) for h in heads]
    n = range(len(items))

    @pl.when(i == 0)
    def _():
        state_ref[...] = jnp.zeros_like(state_ref)

    row = lax.broadcasted_iota(jnp.int32, (c, c), 0)
    col = lax.broadcasted_iota(jnp.int32, (c, c), 1)
    causal = row >= col
    strict = row > col
    eye, diag, levels = _tri_masks(row, col)
    tri_incl = causal.astype(BF16)

    gc_all, gc_t, beta_all, eg_all, egl_all, gl_all = [], [], [], [], [], []
    for ck in range(n_ck):
        ab = ab_ref[0, ck * c:(ck + 1) * c, :]
        x_dt = ab + dtb_ref[...]
        softplus = jnp.maximum(x_dt, 0.0) + jnp.log(1.0 + jnp.exp(-jnp.abs(x_dt)))
        g_hi, g_lo = _split_bf16(-jnp.exp(alog_ref[...]) * softplus)
        gc = _dot(tri_incl, g_hi) + _dot(tri_incl, g_lo)
        g_last = gc[c - 1:c, :]
        gc_all.append(gc)
        gc_t.append(gc.T)
        beta_all.append(_sigmoid(ab))
        eg_all.append(jnp.exp(gc))
        egl_all.append(jnp.exp(g_last - gc))
        gl_all.append(jnp.exp(g_last))

    def section(sec, ck, h):
        lo = sec * width + h * LANES
        return qkv_ref[0, ck * c:(ck + 1) * c, lo:lo + LANES]

    qb = [section(0, ck, h) for ck, h in items]
    kb = [section(1, ck, h) for ck, h in items]
    q = [qb[j].astype(F32) for j in n]
    k = [kb[j].astype(F32) for j in n]
    v = [section(2, ck, h).astype(F32) for ck, h in items]
    decay = []
    for ck, h in items:
        diff = jnp.where(causal, gc_all[ck][:, h:h + 1] - gc_t[ck][h:h + 1, :], 0.0)
        decay.append(jnp.where(causal, jnp.exp(diff), 0.0))
    beta = [beta_all[ck][:, HEADS + h:HEADS + h + 1] for ck, h in items]
    eg = [eg_all[ck][:, h:h + 1] for ck, h in items]

    qkk = [_dot_nt(jnp.concatenate([qb[j], kb[j]], axis=0), kb[j]) for j in n]
    qk = [(qkk[j][:c] * decay[j]).astype(BF16) for j in n]
    a_low = [jnp.where(strict, beta[j] * qkk[j][c:] * decay[j], 0.0) for j in n]

    pw = [jnp.where(diag, -a_low[j], 0.0) for j in n]
    t = [eye + pw[j] for j in n]
    for _ in range(TRI_BASE_LOG2 - 1):
        pw = [_bdot(pw[j], pw[j]) for j in n]
        t = [t[j] + _bdot(t[j], pw[j]) for j in n]
    for lvl in levels:
        y = [_bdot(jnp.where(lvl, a_low[j], 0.0), t[j]) for j in n]
        t = [t[j] - _bdot(t[j], y[j]) for j in n]

    rhs = [jnp.concatenate([v[j] * beta[j], k[j] * (beta[j] * eg[j])], axis=1) for j in n]
    sol = [_bdot(t[j], rhs[j]) for j in n]
    lhs1 = [jnp.concatenate([sol[j][:, LANES:], q[j] * eg[j]], axis=0).astype(BF16) for j in n]
    lhs2 = [jnp.concatenate([qk[j], (k[j] * egl_all[ck][:, h:h + 1]).T.astype(BF16)], axis=0)
            for j, (ck, h) in enumerate(items)]

    ng = ng_ref[...]
    state = [state_ref[h] for h in heads]
    for ck in range(n_ck):
        js = [ck * HEADS + h for h in heads]
        sb = [state[h].astype(BF16) for h in heads]
        r1 = [_dot(lhs1[js[h]], sb[h]) for h in heads]
        vb = [(sol[js[h]][:, :LANES] - r1[h][:c]).astype(BF16) for h in heads]
        r2 = [_dot(lhs2[js[h]], vb[h]) for h in heads]
        state = [state[h] * gl_all[ck][:, h:h + 1] + r2[h][c:] for h in heads]
        for h in heads:
            o = r1[h][c:] + r2[h][:c]
            gate = gate_ref[0, ck * c:(ck + 1) * c, h * LANES:(h + 1) * LANES].astype(F32)
            scale = lax.rsqrt(jnp.mean(o * o, axis=-1, keepdims=True) + EPS)
            o_ref[0, ck * c:(ck + 1) * c, h * LANES:(h + 1) * LANES] = (o * scale * ng * gate).astype(o_ref.dtype)
    for h in heads:
        state_ref[h] = state[h]


def gdn_core(proj, ab, a_log, dt_bias, norm_g):
    b, s, _ = proj.shape
    width = HEADS * LANES
    chunk = GDN_CHUNK * GDN_CHUNKS_PER_STEP
    pad = lambda v: jnp.zeros((1, LANES), F32).at[0, :HEADS].set(v)
    return pl.pallas_call(
        _gdn_kernel,
        grid=(b, s // chunk),
        in_specs=[
            pl.BlockSpec((1, chunk, 3 * width), lambda bi, i: (bi, i, 0)),
            pl.BlockSpec((1, chunk, width), lambda bi, i: (bi, i, 3)),
            pl.BlockSpec((1, chunk, LANES), lambda bi, i: (bi, i, 0)),
            pl.BlockSpec((1, LANES), lambda bi, i: (0, 0)),
            pl.BlockSpec((1, LANES), lambda bi, i: (0, 0)),
            pl.BlockSpec((1, LANES), lambda bi, i: (0, 0)),
        ],
        out_specs=pl.BlockSpec((1, chunk, width), lambda bi, i: (bi, i, 0)),
        out_shape=jax.ShapeDtypeStruct((b, s, width), BF16),
        scratch_shapes=[pltpu.VMEM((HEADS, LANES, LANES), F32)],
        compiler_params=_cparams(("parallel", "arbitrary")),
        name="gdn_core",
    )(proj, proj, ab, pad(a_log), pad(dt_bias), norm_g.reshape(1, LANES))


def _sb_bands(qs, kbs, vbs, upper, carries, valid, diag_mask):
    bands, subs = range(len(kbs)), range(len(qs))
    z = [[_dot_nt(qs[i], kbs[d][i]) for i in subs] for d in bands]
    log_sig, log_not = [], []
    for d in bands:
        ls, ln = [], []
        for i in subs:
            nz = -z[d][i]
            nsp = jnp.minimum(nz, 0.0) - jnp.log(1.0 + jnp.exp(jnp.minimum(z[d][i], nz)))
            ls.append(z[d][i] + nsp)
            ln.append(jnp.where(diag_mask, nsp, 0.0) if (d == 0 and diag_mask is not None) else nsp)
        log_sig.append(ls)
        log_not.append(ln)
    suffix = [[_dot(log_not[d][i].astype(BF16), upper) for i in subs] for d in bands]
    wgt = [[] for _ in subs]
    carry_out = []
    for i in subs:
        carry = carries[i]
        for d in bands:
            if valid[d][i] is not None:
                carry = jnp.where(valid[d][i], carry, NEG_BIG)
            log_w = log_sig[d][i] + suffix[d][i] + carry
            if d == 0 and diag_mask is not None:
                log_w = jnp.where(diag_mask, log_w, -jnp.inf)
            wgt[i].append(jnp.exp(log_w).astype(BF16))
            carry = carry + (suffix[d][i][:, 0:1] + log_not[d][i][:, 0:1])
        carry_out.append(carry)
    out = [_dot(jnp.concatenate(wgt[i], axis=1), jnp.concatenate([vbs[d][i] for d in bands], axis=0))
           for i in subs]
    return out, carry_out


SB_BLK = LANES
SB_STATIC_BANDS = 3
F32_EXP_UNDERFLOW = -104.0
NEG_BIG = -1e30


def _sb_kernel(q_ref, k_ref, v_ref, o_ref, acc_ref, carry_ref):
    gi = pl.program_id(2)
    n_sub = q_ref.shape[1] // SB_BLK
    first_blk = gi * n_sub
    row = lax.broadcasted_iota(jnp.int32, (SB_BLK, SB_BLK), 0)
    col = lax.broadcasted_iota(jnp.int32, (SB_BLK, SB_BLK), 1)
    upper = (row > col).astype(BF16)

    def rows(i):
        return slice(i * SB_BLK, (i + 1) * SB_BLK)

    subs = range(n_sub)

    def bands(d0, n_bands, first):
        qs = [q_ref[0, rows(i), :] for i in subs]
        kbs, vbs, valid = [], [], []
        for d in range(n_bands):
            blk = [first_blk + i - (d0 + d) for i in subs]
            start = [pl.multiple_of(jnp.maximum(blk[i], 0) * SB_BLK, SB_BLK) for i in subs]
            kbs.append([k_ref[0, pl.ds(start[i], SB_BLK), :] for i in subs])
            vbs.append([v_ref[0, pl.ds(start[i], SB_BLK), :] for i in subs])
            valid.append([None if (first and d == 0) else blk[i] >= 0 for i in subs])
        carries = [0.0 if first else carry_ref[:, i:i + 1] for i in subs]
        out, carry_out = _sb_bands(qs, kbs, vbs, upper, carries, valid, col < row if first else None)
        for i in subs:
            if first:
                acc_ref[rows(i), :] = out[i]
            else:
                acc_ref[rows(i), :] += out[i]
            carry_ref[:, i:i + 1] = carry_out[i]

    bands(0, SB_STATIC_BANDS, True)

    def any_row_live():
        return jnp.max(carry_ref[:, 0:n_sub]) >= F32_EXP_UNDERFLOW

    def unfinished(state):
        d, live = state
        return jnp.logical_and(d <= first_blk + n_sub - 1, live)

    def body(state):
        d, _ = state
        bands(d, 1, False)
        return d + 1, any_row_live()

    lax.while_loop(unfinished, body, (jnp.int32(SB_STATIC_BANDS), any_row_live()))
    o_ref[0] = acc_ref[...].astype(o_ref.dtype)


def sb_attention(q, kv, *, tq):
    b, s, width = q.shape
    return pl.pallas_call(
        _sb_kernel,
        grid=(b, HEADS, s // tq),
        in_specs=[
            pl.BlockSpec((1, tq, LANES), lambda bi, h, i: (bi, i, h)),
            pl.BlockSpec((1, s, LANES), lambda bi, h, i: (bi, 0, h)),
            pl.BlockSpec((1, s, LANES), lambda bi, h, i: (bi, 0, HEADS + h)),
        ],
        out_specs=pl.BlockSpec((1, tq, LANES), lambda bi, h, i: (bi, i, h)),
        out_shape=jax.ShapeDtypeStruct((b, s, width), BF16),
        scratch_shapes=[pltpu.VMEM((tq, LANES), F32), pltpu.VMEM((SB_BLK, LANES), F32)],
        compiler_params=_cparams(("parallel", "parallel", "arbitrary")),
        name="sb_attention",
    )(q, kv, kv)


def kernel(x, p, ln_mix, ln_ffn, ln_ple, gdn_w_in, gdn_conv, gdn_a_log, gdn_dt_bias, gdn_norm,
           gdn_w_out, kv_norm, w_kv, k_norm, sb_w_q, sb_q_norm, sb_w_out, ffn_w_in, ffn_w_out,
           ple_w_proj, ple_w_gate):
    b, s, d = x.shape
    depth = p.shape[0]
    n_a = gdn_w_in.shape[0]
    m = b * s
    width = HEADS * LANES

    bf = lambda w: w.astype(BF16)
    gains = lambda g: g.reshape(g.shape[0], 1, g.shape[-1])
    gdn_w_in_b, gdn_w_out_b, sb_w_q_b, sb_w_out_b = bf(gdn_w_in), bf(gdn_w_out), bf(sb_w_q), bf(sb_w_out)
    ffn_w_in_b, ffn_w_out_b, ple_w_proj_b, ple_w_gate_b = bf(ffn_w_in), bf(ffn_w_out), bf(ple_w_proj), bf(ple_w_gate)
    n_ab = gdn_w_in.shape[2] - 4 * width
    gdn_w_ab_b = bf(jnp.pad(gdn_w_in[:, :, 4 * width:], ((0, 0), (0, 0), (0, LANES - n_ab))))
    ln_mix_g, ln_ffn_g, ln_ple_g, sb_q_norm_g = gains(ln_mix), gains(ln_ffn), gains(ln_ple), gains(sb_q_norm)
    p_rows = p.reshape(depth, m, p.shape[-1])

    h = x.reshape(m, d)
    kv = None
    for i in range(depth):
        if i < n_a:
            proj_main, proj_ab = proj_gdn(h, ln_mix_g, gdn_w_in_b, gdn_w_ab_b, gdn_conv, tm=TM, layer=i, seq_len=s)
            o = gdn_core(proj_main.reshape(b, s, 4 * width), proj_ab.reshape(b, s, LANES),
                         gdn_a_log[i], gdn_dt_bias[i], gdn_norm[i])
            w_out, mixer_layer = gdn_w_out_b, i
        else:
            j = i - n_a
            q = proj(h, ln_mix_g, sb_w_q_b, sb_q_norm_g, tm=TM, g_layer=i, w_layer=j, n_normed=width,
                     out_scale=LANES ** -0.5)
            o = sb_attention(q.reshape(b, s, width), kv, tq=8 * SB_BLK)
            w_out, mixer_layer = sb_w_out_b, j
        h = post_mixer(h, o.reshape(m, width), p_rows, w_out, ln_ffn_g, ffn_w_in_b, ffn_w_out_b, ln_ple_g,
                       ple_w_proj_b, ple_w_gate_b, tm=TM, layer=i, mixer_layer=mixer_layer)
        if i == n_a - 1:
            kv = proj(h, kv_norm.reshape(1, d), bf(w_kv), k_norm.reshape(1, LANES), tm=TM,
                      n_normed=width).reshape(b, s, 2 * width)
    return h.reshape(b, s, d)
```

```python
import functools

import jax
import jax.numpy as jnp
from jax import lax
from jax.experimental import pallas as pl
from jax.experimental.pallas import tpu as pltpu

EPS = 1e-6
LANES = 128
HEADS = 8
CONV_W = 4
HALO = 16
VMEM_LIMIT = 56 * 1024 * 1024
TM = 512
FFN_CHUNK = 256

F32 = jnp.float32
BF16 = jnp.bfloat16


def _cparams(sem):
    return pltpu.CompilerParams(dimension_semantics=sem, vmem_limit_bytes=VMEM_LIMIT)


def _dot(a, b):
    return jnp.dot(a, b, preferred_element_type=F32)


def _dot_nt(a, b):
    return lax.dot_general(a, b, (((1,), (1,)), ((), ())), preferred_element_type=F32)


def _rms(x, g):
    return x * lax.rsqrt(jnp.mean(x * x, axis=-1, keepdims=True) + EPS) * g


def _sigmoid(x):
    return 1.0 / (1.0 + jnp.exp(-x))


def _resident(arr, layer=None, cols=None):
    shape = arr.shape if layer is None else arr.shape[1:]
    if cols is not None:
        shape = shape[:-1] + (cols,)
    zeros = (0,) * len(shape)
    if layer is None:
        return pl.BlockSpec(shape, lambda i: zeros, pipeline_mode=pl.Buffered(1))
    return pl.BlockSpec((None,) + shape, lambda i: (layer,) + zeros, pipeline_mode=pl.Buffered(1))


def _proj_kernel(x_ref, g_ref, w_ref, hg_ref, o_ref, *, n_normed, out_scale):
    xn = _rms(x_ref[...], g_ref[...]).astype(BF16)
    hg = hg_ref[...]
    n = w_ref.shape[1]
    step = 2 * LANES
    y_next = _dot(xn, w_ref[:, 0:step])
    for lo in range(0, n, step):
        y = y_next
        if lo + step < n:
            y_next = _dot(xn, w_ref[:, lo + step:lo + 2 * step])
        for c in range(step // LANES):
            col = lo + c * LANES
            yc = y[:, c * LANES:(c + 1) * LANES]
            if col < n_normed:
                yc = _rms(yc, hg) * out_scale
            o_ref[:, col:col + LANES] = yc.astype(o_ref.dtype)


def proj(x, g, w, head_g, *, tm, g_layer=None, w_layer=None, n_normed=0, out_scale=1.0):
    m, d = x.shape
    n = w.shape[-1]
    kern = functools.partial(_proj_kernel, n_normed=n_normed, out_scale=out_scale)
    return pl.pallas_call(
        kern,
        grid=(m // tm,),
        in_specs=[pl.BlockSpec((tm, d), lambda i: (i, 0)), _resident(g, g_layer), _resident(w, w_layer),
                  _resident(head_g, w_layer)],
        out_specs=pl.BlockSpec((tm, n), lambda i: (i, 0)),
        out_shape=jax.ShapeDtypeStruct((m, n), BF16),
        compiler_params=_cparams(("parallel",)),
        name="proj",
    )(x, g, w, head_g)


def _proj_gdn_kernel(x_ref, xh_ref, g_ref, w_ref, wab_ref, conv_ref, o_ref, ab_ref, *, tiles_per_seq):
    i = pl.program_id(0)
    width = HEADS * LANES
    g = g_ref[...]
    xn = _rms(x_ref[...], g).astype(BF16)
    xa = jnp.concatenate([_rms(xh_ref[...], g).astype(BF16), xn], axis=0)
    seq_start = i % tiles_per_seq == 0
    ones = jnp.ones((LANES, LANES), BF16)
    step = 4 * LANES
    chunks = list(range(0, 3 * width, step))
    y_next = _dot(xa, w_ref[:, 0:step])
    for lo in chunks:
        y = jnp.concatenate([jnp.where(seq_start, 0.0, y_next[:HALO]), y_next[HALO:]], axis=0)
        if lo + step < 3 * width:
            y_next = _dot(xa, w_ref[:, lo + step:lo + 2 * step])
        tap = lambda k: conv_ref[CONV_W - 1 - k:CONV_W - k, lo:lo + step]
        y1 = pltpu.roll(y, 1, 0)
        acc = (tap(0) * y + tap(1) * y1 + pltpu.roll(tap(2) * y + tap(3) * y1, 2, 0))[HALO:]
        act = acc * _sigmoid(acc)
        for c in range(step // LANES):
            col = lo + c * LANES
            a = act[:, c * LANES:(c + 1) * LANES]
            if col < 2 * width:
                a = a * lax.rsqrt(_dot((a * a).astype(BF16), ones) + EPS)
                if col < width:
                    a = a * (LANES ** -0.5)
            o_ref[:, col:col + LANES] = a.astype(o_ref.dtype)
    for lo in range(3 * width, 4 * width, step):
        y = _dot(xn, w_ref[:, lo:lo + step])
        o_ref[:, lo:lo + step] = (y * _sigmoid(y)).astype(o_ref.dtype)
    ab_ref[...] = _dot(xn, wab_ref[...])


def proj_gdn(x, g, w_in, w_ab, conv_w, *, tm, layer, seq_len):
    m, d = x.shape
    n = 4 * HEADS * LANES
    halo_blocks = tm // HALO
    kern = functools.partial(_proj_gdn_kernel, tiles_per_seq=seq_len // tm)
    return pl.pallas_call(
        kern,
        grid=(m // tm,),
        in_specs=[
            pl.BlockSpec((tm, d), lambda i: (i, 0)),
            pl.BlockSpec((HALO, d), lambda i: (jnp.maximum(i * halo_blocks - 1, 0), 0)),
            _resident(g, layer), _resident(w_in, layer, cols=n), _resident(w_ab, layer),
            _resident(conv_w, layer),
        ],
        out_specs=[pl.BlockSpec((tm, n), lambda i: (i, 0)), pl.BlockSpec((tm, LANES), lambda i: (i, 0))],
        out_shape=[jax.ShapeDtypeStruct((m, n), BF16), jax.ShapeDtypeStruct((m, LANES), F32)],
        compiler_params=_cparams(("parallel",)),
        name="proj_gdn",
    )(x, x, g, w_in, w_ab, conv_w)


def _post_kernel(h_ref, o_ref, p_ref, wo_ref, gf_ref, w1_ref, w2_ref, gp_ref, wp_ref, wg_ref, out_ref,
                 hn_ref, acc_ref):
    hidden = w2_ref.shape[0]
    h1 = h_ref[...] + _dot(o_ref[...], wo_ref[...])
    hn_ref[...] = _rms(h1, gf_ref[...]).astype(BF16)
    acc_ref[...] = h1

    def gate_and_linear(lo):
        hn = hn_ref[...]
        return (_dot(hn, w1_ref[:, lo:lo + FFN_CHUNK]),
                _dot(hn, w1_ref[:, hidden + lo:hidden + lo + FFN_CHUNK]))

    a, b = gate_and_linear(0)
    for lo in range(0, hidden, FFN_CHUNK):
        act = (a * _sigmoid(a) * b).astype(BF16)
        if lo + FFN_CHUNK < hidden:
            a, b = gate_and_linear(lo + FFN_CHUNK)
        acc_ref[...] += _dot(act, w2_ref[lo:lo + FFN_CHUNK, :])
    h2 = acc_ref[...]
    gate = _sigmoid(_dot(_rms(h2, gp_ref[...]).astype(BF16), wg_ref[...]))
    emb = _dot(p_ref[...].astype(BF16), wp_ref[...])
    out_ref[...] = h2 + emb * gate


def post_mixer(h, o, p, w_out, g_ffn, w1, w2, g_ple, wp, wg, *, tm, layer, mixer_layer):
    m, d = h.shape
    return pl.pallas_call(
        _post_kernel,
        grid=(m // tm,),
        in_specs=[
            pl.BlockSpec((tm, d), lambda i: (i, 0)),
            pl.BlockSpec((tm, o.shape[1]), lambda i: (i, 0)),
            pl.BlockSpec((None, tm, p.shape[2]), lambda i: (layer, i, 0)),
            _resident(w_out, mixer_layer), _resident(g_ffn, layer), _resident(w1, layer), _resident(w2, layer),
            _resident(g_ple, layer), _resident(wp, layer), _resident(wg, layer),
        ],
        out_specs=pl.BlockSpec((tm, d), lambda i: (i, 0)),
        out_shape=jax.ShapeDtypeStruct((m, d), F32),
        scratch_shapes=[pltpu.VMEM((tm, d), BF16), pltpu.VMEM((tm, d), F32)],
        compiler_params=_cparams(("parallel",)),
        name="post_mixer",
    )(h, o, p, w_out, g_ffn, w1, w2, g_ple, wp, wg)


TRI_BASE_LOG2 = 4
GDN_CHUNK = LANES
GDN_CHUNKS_PER_STEP = 4


def _tri_masks(row, col):
    c = row.shape[0]
    eye = (row == col).astype(F32)
    diag = (row >> TRI_BASE_LOG2) == (col >> TRI_BASE_LOG2)
    levels = []
    sh = TRI_BASE_LOG2
    while (1 << sh) < c:
        rb = row >> sh
        levels.append(((rb & 1) == 1) & ((col >> sh) == rb - 1))
        sh += 1
    return eye, diag, levels


def _bdot(a, b):
    return _dot(a.astype(BF16), b.astype(BF16))


def _split_bf16(x):
    hi = x.astype(BF16)
    return hi, (x - hi.astype(F32)).astype(BF16)


def _gdn_kernel(qkv_ref, gate_ref, ab_ref, alog_ref, dtb_ref, ng_ref, o_ref, state_ref):
    i = pl.program_id(1)
    c = GDN_CHUNK
    n_ck = qkv_ref.shape[1] // c
    width = HEADS * LANES
    heads = range(HEADS)
    items = [(ck, h) for ck in range(n_ck) for h in heads]
    n = range(len(items))

    @pl.when(i == 0)
    def _():
        state_ref[...] = jnp.zeros_like(state_ref)

    row = lax.broadcasted_iota(jnp.int32, (c, c), 0)
    col = lax.broadcasted_iota(jnp.int32, (c, c), 1)
    causal = row >= col
    strict = row > col
    eye, diag, levels = _tri_masks(row, col)
    tri_incl = causal.astype(BF16)

    gc_all, gc_t, beta_all, eg_all, egl_all, gl_all = [], [], [], [], [], []
    for ck in range(n_ck):
        ab = ab_ref[0, ck * c:(ck + 1) * c, :]
        x_dt = ab + dtb_ref[...]
        softplus = jnp.maximum(x_dt, 0.0) + jnp.log(1.0 + jnp.exp(-jnp.abs(x_dt)))
        g_hi, g_lo = _split_bf16(-jnp.exp(alog_ref[...]) * softplus)
        gc = _dot(tri_incl, g_hi) + _dot(tri_incl, g_lo)
        g_last = gc[c - 1:c, :]
        gc_all.append(gc)
        gc_t.append(gc.T)
        beta_all.append(_sigmoid(ab))
        eg_all.append(jnp.exp(gc))
        egl_all.append(jnp.exp(g_last - gc))
        gl_all.append(jnp.exp(g_last))

    def section(sec, ck, h):
        lo = sec * width + h * LANES
        return qkv_ref[0, ck * c:(ck + 1) * c, lo:lo + LANES]

    qb = [section(0, ck, h) for ck, h in items]
    kb = [section(1, ck, h) for ck, h in items]
    q = [qb[j].astype(F32) for j in n]
    k = [kb[j].astype(F32) for j in n]
    v = [section(2, ck, h).astype(F32) for ck, h in items]
    decay = []
    for ck, h in items:
        diff = jnp.where(causal, gc_all[ck][:, h:h + 1] - gc_t[ck][h:h + 1, :], 0.0)
        decay.append(jnp.where(causal, jnp.exp(diff), 0.0))
    beta = [beta_all[ck][:, HEADS + h:HEADS + h + 1] for ck, h in items]
    eg = [eg_all[ck][:, h:h + 1] for ck, h in items]

    qkk = [_dot_nt(jnp.concatenate([qb[j], kb[j]], axis=0), kb[j]) for j in n]
    qk = [(qkk[j][:c] * decay[j]).astype(BF16) for j in n]
    a_low = [jnp.where(strict, beta[j] * qkk[j][c:] * decay[j], 0.0) for j in n]

    pw = [jnp.where(diag, -a_low[j], 0.0) for j in n]
    t = [eye + pw[j] for j in n]
    for _ in range(TRI_BASE_LOG2 - 1):
        pw = [_bdot(pw[j], pw[j]) for j in n]
        t = [t[j] + _bdot(t[j], pw[j]) for j in n]
    for lvl in levels:
        y = [_bdot(jnp.where(lvl, a_low[j], 0.0), t[j]) for j in n]
        t = [t[j] - _bdot(t[j], y[j]) for j in n]

    rhs = [jnp.concatenate([v[j] * beta[j], k[j] * (beta[j] * eg[j])], axis=1) for j in n]
    sol = [_bdot(t[j], rhs[j]) for j in n]
    lhs1 = [jnp.concatenate([sol[j][:, LANES:], q[j] * eg[j]], axis=0).astype(BF16) for j in n]
    lhs2 = [jnp.concatenate([qk[j], (k[j] * egl_all[ck][:, h:h + 1]).T.astype(BF16)], axis=0)
            for j, (ck, h) in enumerate(items)]

    ng = ng_ref[...]
    state = [state_ref[h] for h in heads]
    for ck in range(n_ck):
        js = [ck * HEADS + h for h in heads]
        sb = [state[h].astype(BF16) for h in heads]
        r1 = [_dot(lhs1[js[h]], sb[h]) for h in heads]
        vb = [(sol[js[h]][:, :LANES] - r1[h][:c]).astype(BF16) for h in heads]
        r2 = [_dot(lhs2[js[h]], vb[h]) for h in heads]
        state = [state[h] * gl_all[ck][:, h:h + 1] + r2[h][c:] for h in heads]
        for h in heads:
            o = r1[h][c:] + r2[h][:c]
            gate = gate_ref[0, ck * c:(ck + 1) * c, h * LANES:(h + 1) * LANES].astype(F32)
            scale = lax.rsqrt(jnp.mean(o * o, axis=-1, keepdims=True) + EPS)
            o_ref[0, ck * c:(ck + 1) * c, h * LANES:(h + 1) * LANES] = (o * scale * ng * gate).astype(o_ref.dtype)
    for h in heads:
        state_ref[h] = state[h]


def gdn_core(proj, ab, a_log, dt_bias, norm_g):
    b, s, _ = proj.shape
    width = HEADS * LANES
    chunk = GDN_CHUNK * GDN_CHUNKS_PER_STEP
    pad = lambda v: jnp.zeros((1, LANES), F32).at[0, :HEADS].set(v)
    return pl.pallas_call(
        _gdn_kernel,
        grid=(b, s // chunk),
        in_specs=[
            pl.BlockSpec((1, chunk, 3 * width), lambda bi, i: (bi, i, 0)),
            pl.BlockSpec((1, chunk, width), lambda bi, i: (bi, i, 3)),
            pl.BlockSpec((1, chunk, LANES), lambda bi, i: (bi, i, 0)),
            pl.BlockSpec((1, LANES), lambda bi, i: (0, 0)),
            pl.BlockSpec((1, LANES), lambda bi, i: (0, 0)),
            pl.BlockSpec((1, LANES), lambda bi, i: (0, 0)),
        ],
        out_specs=pl.BlockSpec((1, chunk, width), lambda bi, i: (bi, i, 0)),
        out_shape=jax.ShapeDtypeStruct((b, s, width), BF16),
        scratch_shapes=[pltpu.VMEM((HEADS, LANES, LANES), F32)],
        compiler_params=_cparams(("parallel", "arbitrary")),
        name="gdn_core",
    )(proj, proj, ab, pad(a_log), pad(dt_bias), norm_g.reshape(1, LANES))


SB_SUB = 64
SB_KEYS = LANES
SB_TQ = 1024
SB_STATIC_BANDS = 2
F32_EXP_UNDERFLOW = -104.0
NEG_BIG = -1e30


def _sb_bands(qs, kbs, vbs, upper, carries, key_limit, row_offset):
    bands, subs = range(len(kbs)), range(len(qs))
    rows, keys = qs[0].shape[0], kbs[0][0].shape[0]
    row = lax.broadcasted_iota(jnp.int32, (rows, keys), 0)
    col = lax.broadcasted_iota(jnp.int32, (rows, keys), 1)
    z = [[_dot_nt(qs[i], kbs[d][i]) for i in subs] for d in bands]
    log_sig, log_not = [], []
    for d in bands:
        ls, ln = [], []
        for i in subs:
            bound = key_limit[d][i] + row if row_offset[d] else key_limit[d][i]
            zm = jnp.where(col < bound, z[d][i], NEG_BIG)
            nz = -zm
            nsp = jnp.minimum(nz, 0.0) - jnp.log(1.0 + jnp.exp(jnp.minimum(zm, nz)))
            ls.append(zm + nsp)
            ln.append(nsp)
        log_sig.append(ls)
        log_not.append(ln)
    suffix = [[_dot(log_not[d][i].astype(BF16), upper) for i in subs] for d in bands]
    wgt = [[] for _ in subs]
    carry_out = []
    for i in subs:
        carry = carries[i]
        for d in bands:
            wgt[i].append(jnp.exp(log_sig[d][i] + suffix[d][i] + carry).astype(BF16))
            carry = carry + (suffix[d][i][:, 0:1] + log_not[d][i][:, 0:1])
        carry_out.append(carry)
    out = [_dot(jnp.concatenate(wgt[i], axis=1), jnp.concatenate([vbs[d][i] for d in bands], axis=0))
           for i in subs]
    return out, carry_out


def _sb_kernel(q_ref, k_ref, v_ref, o_ref, acc_ref, carry_ref):
    gi = pl.program_id(2)
    tq = q_ref.shape[1]
    n_sub = tq // SB_SUB
    subs = range(n_sub)
    r2 = lax.broadcasted_iota(jnp.int32, (SB_KEYS, SB_KEYS), 0)
    c2 = lax.broadcasted_iota(jnp.int32, (SB_KEYS, SB_KEYS), 1)
    upper = (r2 > c2).astype(BF16)

    def rows(i):
        return slice(i * SB_SUB, (i + 1) * SB_SUB)

    q_start = [gi * tq + i * SB_SUB for i in subs]

    def bands(d0, n_bands, first):
        qs = [q_ref[0, rows(i), :] for i in subs]
        kbs, vbs, limit, offs = [], [], [], []
        for d in range(n_bands):
            band_end = [q_start[i] + SB_SUB - (d0 + d) * SB_KEYS for i in subs]
            start = [pl.multiple_of(jnp.maximum(band_end[i] - SB_KEYS, 0), SB_SUB) for i in subs]
            kbs.append([k_ref[0, pl.ds(start[i], SB_KEYS), :] for i in subs])
            vbs.append([v_ref[0, pl.ds(start[i], SB_KEYS), :] for i in subs])
            causal = first and d == 0
            limit.append([(q_start[i] if causal else band_end[i]) - start[i] for i in subs])
            offs.append(causal)
        carries = [0.0 if first else carry_ref[:, i:i + 1] for i in subs]
        out, carry_out = _sb_bands(qs, kbs, vbs, upper, carries, limit, offs)
        for i in subs:
            if first:
                acc_ref[rows(i), :] = out[i]
            else:
                acc_ref[rows(i), :] += out[i]
            carry_ref[:, i:i + 1] = carry_out[i]

    bands(0, SB_STATIC_BANDS, True)

    def any_row_live():
        return jnp.max(carry_ref[:, 0:n_sub]) >= F32_EXP_UNDERFLOW

    def unfinished(state):
        d, live = state
        return jnp.logical_and((gi + 1) * tq - d * SB_KEYS > 0, live)

    def body(state):
        d, _ = state
        bands(d, 1, False)
        return d + 1, any_row_live()

    lax.while_loop(unfinished, body, (jnp.int32(SB_STATIC_BANDS), any_row_live()))
    o_ref[0] = acc_ref[...].astype(o_ref.dtype)


def sb_attention(q, kv, *, tq):
    b, s, width = q.shape
    return pl.pallas_call(
        _sb_kernel,
        grid=(b, HEADS, s // tq),
        in_specs=[
            pl.BlockSpec((1, tq, LANES), lambda bi, h, i: (bi, i, h)),
            pl.BlockSpec((1, s, LANES), lambda bi, h, i: (bi, 0, h)),
            pl.BlockSpec((1, s, LANES), lambda bi, h, i: (bi, 0, HEADS + h)),
        ],
        out_specs=pl.BlockSpec((1, tq, LANES), lambda bi, h, i: (bi, i, h)),
        out_shape=jax.ShapeDtypeStruct((b, s, width), BF16),
        scratch_shapes=[pltpu.VMEM((tq, LANES), F32), pltpu.VMEM((SB_SUB, LANES), F32)],
        compiler_params=_cparams(("parallel", "parallel", "arbitrary")),
        name="sb_attention",
    )(q, kv, kv)


def kernel(x, p, ln_mix, ln_ffn, ln_ple, gdn_w_in, gdn_conv, gdn_a_log, gdn_dt_bias, gdn_norm,
           gdn_w_out, kv_norm, w_kv, k_norm, sb_w_q, sb_q_norm, sb_w_out, ffn_w_in, ffn_w_out,
           ple_w_proj, ple_w_gate):
    b, s, d = x.shape
    depth = p.shape[0]
    n_a = gdn_w_in.shape[0]
    m = b * s
    width = HEADS * LANES

    bf = lambda w: w.astype(BF16)
    gains = lambda g: g.reshape(g.shape[0], 1, g.shape[-1])
    gdn_w_in_b = bf(gdn_w_in[:, :, :4 * width])
    gdn_w_out_b, sb_w_q_b, sb_w_out_b = bf(gdn_w_out), bf(sb_w_q), bf(sb_w_out)
    ffn_w_in_b, ffn_w_out_b, ple_w_proj_b, ple_w_gate_b = bf(ffn_w_in), bf(ffn_w_out), bf(ple_w_proj), bf(ple_w_gate)
    n_ab = gdn_w_in.shape[2] - 4 * width
    gdn_w_ab_b = bf(jnp.pad(gdn_w_in[:, :, 4 * width:], ((0, 0), (0, 0), (0, LANES - n_ab))))
    ln_mix_g, ln_ffn_g, ln_ple_g, sb_q_norm_g = gains(ln_mix), gains(ln_ffn), gains(ln_ple), gains(sb_q_norm)
    p_rows = p.reshape(depth, m, p.shape[-1])

    h = x.reshape(m, d)
    kv = None
    for i in range(depth):
        if i < n_a:
            proj_main, proj_ab = proj_gdn(h, ln_mix_g, gdn_w_in_b, gdn_w_ab_b, gdn_conv, tm=TM, layer=i, seq_len=s)
            o = gdn_core(proj_main.reshape(b, s, 4 * width), proj_ab.reshape(b, s, LANES),
                         gdn_a_log[i], gdn_dt_bias[i], gdn_norm[i])
            w_out, mixer_layer = gdn_w_out_b, i
        else:
            j = i - n_a
            q = proj(h, ln_mix_g, sb_w_q_b, sb_q_norm_g, tm=TM, g_layer=i, w_layer=j, n_normed=width,
                     out_scale=LANES ** -0.5)
            o = sb_attention(q.reshape(b, s, width), kv, tq=SB_TQ)
            w_out, mixer_layer = sb_w_out_b, j
        h = post_mixer(h, o.reshape(m, width), p_rows, w_out, ln_ffn_g, ffn_w_in_b, ffn_w_out_b, ln_ple_g,
                       ple_w_proj_b, ple_w_gate_b, tm=TM, layer=i, mixer_layer=mixer_layer)
        if i == n_a - 1:
            kv = proj(h, kv_norm.reshape(1, d), bf(w_kv), k_norm.reshape(1, LANES), tm=TM,
                      n_normed=width).reshape(b, s, 2 * width)
    return h.reshape(b, s, d)
```

```python
import functools

import jax
import jax.numpy as jnp
from jax import lax
from jax.experimental import pallas as pl
from jax.experimental.pallas import tpu as pltpu

EPS = 1e-6
LANES = 128
HEADS = 8
CONV_W = 4
HALO = 16
VMEM_LIMIT = 56 * 1024 * 1024
TM = 512
FFN_CHUNK = 256

F32 = jnp.float32
BF16 = jnp.bfloat16


def _cparams(sem):
    return pltpu.CompilerParams(dimension_semantics=sem, vmem_limit_bytes=VMEM_LIMIT)


def _dot(a, b):
    return jnp.dot(a, b, preferred_element_type=F32)


def _dot_nt(a, b):
    return lax.dot_general(a, b, (((1,), (1,)), ((), ())), preferred_element_type=F32)


def _rms(x, g):
    return x * lax.rsqrt(jnp.mean(x * x, axis=-1, keepdims=True) + EPS) * g


def _sigmoid(x):
    return 1.0 / (1.0 + jnp.exp(-x))


def _resident(arr, layer=None, cols=None):
    shape = arr.shape if layer is None else arr.shape[1:]
    if cols is not None:
        shape = shape[:-1] + (cols,)
    zeros = (0,) * len(shape)
    if layer is None:
        return pl.BlockSpec(shape, lambda i: zeros, pipeline_mode=pl.Buffered(1))
    return pl.BlockSpec((None,) + shape, lambda i: (layer,) + zeros, pipeline_mode=pl.Buffered(1))


def _proj_kernel(x_ref, g_ref, w_ref, hg_ref, o_ref, *, n_normed, out_scale):
    xn = _rms(x_ref[...], g_ref[...]).astype(BF16)
    hg = hg_ref[...]
    n = w_ref.shape[1]
    step = 2 * LANES
    y_next = _dot(xn, w_ref[:, 0:step])
    for lo in range(0, n, step):
        y = y_next
        if lo + step < n:
            y_next = _dot(xn, w_ref[:, lo + step:lo + 2 * step])
        for c in range(step // LANES):
            col = lo + c * LANES
            yc = y[:, c * LANES:(c + 1) * LANES]
            if col < n_normed:
                yc = _rms(yc, hg) * out_scale
            o_ref[:, col:col + LANES] = yc.astype(o_ref.dtype)


def proj(x, g, w, head_g, *, tm, g_layer=None, w_layer=None, n_normed=0, out_scale=1.0):
    m, d = x.shape
    n = w.shape[-1]
    kern = functools.partial(_proj_kernel, n_normed=n_normed, out_scale=out_scale)
    return pl.pallas_call(
        kern,
        grid=(m // tm,),
        in_specs=[pl.BlockSpec((tm, d), lambda i: (i, 0)), _resident(g, g_layer), _resident(w, w_layer),
                  _resident(head_g, w_layer)],
        out_specs=pl.BlockSpec((tm, n), lambda i: (i, 0)),
        out_shape=jax.ShapeDtypeStruct((m, n), BF16),
        compiler_params=_cparams(("parallel",)),
        name="proj",
    )(x, g, w, head_g)


def _proj_gdn_kernel(x_ref, xh_ref, g_ref, w_ref, wab_ref, conv_ref, o_ref, ab_ref, *, tiles_per_seq):
    i = pl.program_id(0)
    width = HEADS * LANES
    g = g_ref[...]
    xn = _rms(x_ref[...], g).astype(BF16)
    xa = jnp.concatenate([_rms(xh_ref[...], g).astype(BF16), xn], axis=0)
    seq_start = i % tiles_per_seq == 0
    ones = jnp.ones((LANES, LANES), BF16)
    step = 4 * LANES
    chunks = list(range(0, 3 * width, step))
    y_next = _dot(xa, w_ref[:, 0:step])
    for lo in chunks:
        y = jnp.concatenate([jnp.where(seq_start, 0.0, y_next[:HALO]), y_next[HALO:]], axis=0)
        if lo + step < 3 * width:
            y_next = _dot(xa, w_ref[:, lo + step:lo + 2 * step])
        tap = lambda k: conv_ref[CONV_W - 1 - k:CONV_W - k, lo:lo + step]
        y1 = pltpu.roll(y, 1, 0)
        acc = (tap(0) * y + tap(1) * y1 + pltpu.roll(tap(2) * y + tap(3) * y1, 2, 0))[HALO:]
        act = acc * _sigmoid(acc)
        for c in range(step // LANES):
            col = lo + c * LANES
            a = act[:, c * LANES:(c + 1) * LANES]
            if col < 2 * width:
                a = a * lax.rsqrt(_dot((a * a).astype(BF16), ones) + EPS)
                if col < width:
                    a = a * (LANES ** -0.5)
            o_ref[:, col:col + LANES] = a.astype(o_ref.dtype)
    for lo in range(3 * width, 4 * width, step):
        y = _dot(xn, w_ref[:, lo:lo + step])
        o_ref[:, lo:lo + step] = (y * _sigmoid(y)).astype(o_ref.dtype)
    ab_ref[...] = _dot(xn, wab_ref[...])


def proj_gdn(x, g, w_in, w_ab, conv_w, *, tm, layer, seq_len):
    m, d = x.shape
    n = 4 * HEADS * LANES
    halo_blocks = tm // HALO
    kern = functools.partial(_proj_gdn_kernel, tiles_per_seq=seq_len // tm)
    return pl.pallas_call(
        kern,
        grid=(m // tm,),
        in_specs=[
            pl.BlockSpec((tm, d), lambda i: (i, 0)),
            pl.BlockSpec((HALO, d), lambda i: (jnp.maximum(i * halo_blocks - 1, 0), 0)),
            _resident(g, layer), _resident(w_in, layer, cols=n), _resident(w_ab, layer),
            _resident(conv_w, layer),
        ],
        out_specs=[pl.BlockSpec((tm, n), lambda i: (i, 0)), pl.BlockSpec((tm, LANES), lambda i: (i, 0))],
        out_shape=[jax.ShapeDtypeStruct((m, n), BF16), jax.ShapeDtypeStruct((m, LANES), F32)],
        compiler_params=_cparams(("parallel",)),
        name="proj_gdn",
    )(x, x, g, w_in, w_ab, conv_w)


def _post_kernel(h_ref, o_ref, p_ref, wo_ref, gf_ref, w1_ref, w2_ref, gp_ref, wp_ref, wg_ref, out_ref,
                 hn_ref, acc_ref):
    hidden = w2_ref.shape[0]
    h1 = h_ref[...] + _dot(o_ref[...], wo_ref[...])
    hn_ref[...] = _rms(h1, gf_ref[...]).astype(BF16)
    acc_ref[...] = h1

    def gate_and_linear(lo):
        hn = hn_ref[...]
        return (_dot(hn, w1_ref[:, lo:lo + FFN_CHUNK]),
                _dot(hn, w1_ref[:, hidden + lo:hidden + lo + FFN_CHUNK]))

    a, b = gate_and_linear(0)
    for lo in range(0, hidden, FFN_CHUNK):
        act = (a * _sigmoid(a) * b).astype(BF16)
        if lo + FFN_CHUNK < hidden:
            a, b = gate_and_linear(lo + FFN_CHUNK)
        acc_ref[...] += _dot(act, w2_ref[lo:lo + FFN_CHUNK, :])
    h2 = acc_ref[...]
    gate = _sigmoid(_dot(_rms(h2, gp_ref[...]).astype(BF16), wg_ref[...]))
    emb = _dot(p_ref[...].astype(BF16), wp_ref[...])
    out_ref[...] = h2 + emb * gate


def post_mixer(h, o, p, w_out, g_ffn, w1, w2, g_ple, wp, wg, *, tm, layer, mixer_layer):
    m, d = h.shape
    return pl.pallas_call(
        _post_kernel,
        grid=(m // tm,),
        in_specs=[
            pl.BlockSpec((tm, d), lambda i: (i, 0)),
            pl.BlockSpec((tm, o.shape[1]), lambda i: (i, 0)),
            pl.BlockSpec((None, tm, p.shape[2]), lambda i: (layer, i, 0)),
            _resident(w_out, mixer_layer), _resident(g_ffn, layer), _resident(w1, layer), _resident(w2, layer),
            _resident(g_ple, layer), _resident(wp, layer), _resident(wg, layer),
        ],
        out_specs=pl.BlockSpec((tm, d), lambda i: (i, 0)),
        out_shape=jax.ShapeDtypeStruct((m, d), F32),
        scratch_shapes=[pltpu.VMEM((tm, d), BF16), pltpu.VMEM((tm, d), F32)],
        compiler_params=_cparams(("parallel",)),
        name="post_mixer",
    )(h, o, p, w_out, g_ffn, w1, w2, g_ple, wp, wg)


TRI_BASE_LOG2 = 4
GDN_CHUNK = LANES
GDN_CHUNKS_PER_STEP = 4


def _tri_masks(row, col):
    c = row.shape[0]
    eye = (row == col).astype(F32)
    diag = (row >> TRI_BASE_LOG2) == (col >> TRI_BASE_LOG2)
    levels = []
    sh = TRI_BASE_LOG2
    while (1 << sh) < c:
        rb = row >> sh
        levels.append(((rb & 1) == 1) & ((col >> sh) == rb - 1))
        sh += 1
    return eye, diag, levels


def _bdot(a, b):
    return _dot(a.astype(BF16), b.astype(BF16))


def _split_bf16(x):
    hi = x.astype(BF16)
    return hi, (x - hi.astype(F32)).astype(BF16)


def _gdn_kernel(qkv_ref, gate_ref, ab_ref, alog_ref, dtb_ref, ng_ref, o_ref, state_ref):
    i = pl.program_id(1)
    c = GDN_CHUNK
    n_ck = qkv_ref.shape[1] // c
    width = HEADS * LANES
    heads = range(HEADS)
    items = [(ck, h) for ck in range(n_ck) for h in heads]
    n = range(len(items))

    @pl.when(i == 0)
    def _():
        state_ref[...] = jnp.zeros_like(state_ref)

    row = lax.broadcasted_iota(jnp.int32, (c, c), 0)
    col = lax.broadcasted_iota(jnp.int32, (c, c), 1)
    causal = row >= col
    strict = row > col
    eye, diag, levels = _tri_masks(row, col)
    tri_incl = causal.astype(BF16)

    gc_all, gc_t, beta_all, eg_all, egl_all, gl_all = [], [], [], [], [], []
    for ck in range(n_ck):
        ab = ab_ref[0, ck * c:(ck + 1) * c, :]
        x_dt = ab + dtb_ref[...]
        softplus = jnp.maximum(x_dt, 0.0) + jnp.log(1.0 + jnp.exp(-jnp.abs(x_dt)))
        g_hi, g_lo = _split_bf16(-jnp.exp(alog_ref[...]) * softplus)
        gc = _dot(tri_incl, g_hi) + _dot(tri_incl, g_lo)
        g_last = gc[c - 1:c, :]
        gc_all.append(gc)
        gc_t.append(gc.T)
        beta_all.append(_sigmoid(ab))
        eg_all.append(jnp.exp(gc))
        egl_all.append(jnp.exp(g_last - gc))
        gl_all.append(jnp.exp(g_last))

    def section(sec, ck, h):
        lo = sec * width + h * LANES
        return qkv_ref[0, ck * c:(ck + 1) * c, lo:lo + LANES]

    qb = [section(0, ck, h) for ck, h in items]
    kb = [section(1, ck, h) for ck, h in items]
    q = [qb[j].astype(F32) for j in n]
    k = [kb[j].astype(F32) for j in n]
    v = [section(2, ck, h).astype(F32) for ck, h in items]
    decay = []
    for ck, h in items:
        diff = jnp.where(causal, gc_all[ck][:, h:h + 1] - gc_t[ck][h:h + 1, :], 0.0)
        decay.append(jnp.where(causal, jnp.exp(diff), 0.0))
    beta = [beta_all[ck][:, HEADS + h:HEADS + h + 1] for ck, h in items]
    eg = [eg_all[ck][:, h:h + 1] for ck, h in items]

    qkk = [_dot_nt(jnp.concatenate([qb[j], kb[j]], axis=0), kb[j]) for j in n]
    qk = [(qkk[j][:c] * decay[j]).astype(BF16) for j in n]
    a_low = [jnp.where(strict, beta[j] * qkk[j][c:] * decay[j], 0.0) for j in n]

    pw = [jnp.where(diag, -a_low[j], 0.0) for j in n]
    t = [eye + pw[j] for j in n]
    for _ in range(TRI_BASE_LOG2 - 1):
        pw = [_bdot(pw[j], pw[j]) for j in n]
        t = [t[j] + _bdot(t[j], pw[j]) for j in n]
    for lvl in levels:
        y = [_bdot(jnp.where(lvl, a_low[j], 0.0), t[j]) for j in n]
        t = [t[j] - _bdot(t[j], y[j]) for j in n]

    rhs = [jnp.concatenate([v[j] * beta[j], k[j] * (beta[j] * eg[j])], axis=1) for j in n]
    sol = [_bdot(t[j], rhs[j]) for j in n]
    lhs1 = [jnp.concatenate([sol[j][:, LANES:], q[j] * eg[j]], axis=0).astype(BF16) for j in n]
    lhs2 = [jnp.concatenate([qk[j], (k[j] * egl_all[ck][:, h:h + 1]).T.astype(BF16)], axis=0)
            for j, (ck, h) in enumerate(items)]

    ng = ng_ref[...]
    state = [state_ref[h] for h in heads]
    for ck in range(n_ck):
        js = [ck * HEADS + h for h in heads]
        sb = [state[h].astype(BF16) for h in heads]
        r1 = [_dot(lhs1[js[h]], sb[h]) for h in heads]
        vb = [(sol[js[h]][:, :LANES] - r1[h][:c]).astype(BF16) for h in heads]
        r2 = [_dot(lhs2[js[h]], vb[h]) for h in heads]
        state = [state[h] * gl_all[ck][:, h:h + 1] + r2[h][c:] for h in heads]
        for h in heads:
            o = r1[h][c:] + r2[h][:c]
            gate = gate_ref[0, ck * c:(ck + 1) * c, h * LANES:(h + 1) * LANES].astype(F32)
            scale = lax.rsqrt(jnp.mean(o * o, axis=-1, keepdims=True) + EPS)
            o_ref[0, ck * c:(ck + 1) * c, h * LANES:(h + 1) * LANES] = (o * scale * ng * gate).astype(o_ref.dtype)
    for h in heads:
        state_ref[h] = state[h]


def gdn_core(proj, ab, a_log, dt_bias, norm_g):
    b, s, _ = proj.shape
    width = HEADS * LANES
    chunk = GDN_CHUNK * GDN_CHUNKS_PER_STEP
    pad = lambda v: jnp.zeros((1, LANES), F32).at[0, :HEADS].set(v)
    return pl.pallas_call(
        _gdn_kernel,
        grid=(b, s // chunk),
        in_specs=[
            pl.BlockSpec((1, chunk, 3 * width), lambda bi, i: (bi, i, 0)),
            pl.BlockSpec((1, chunk, width), lambda bi, i: (bi, i, 3)),
            pl.BlockSpec((1, chunk, LANES), lambda bi, i: (bi, i, 0)),
            pl.BlockSpec((1, LANES), lambda bi, i: (0, 0)),
            pl.BlockSpec((1, LANES), lambda bi, i: (0, 0)),
            pl.BlockSpec((1, LANES), lambda bi, i: (0, 0)),
        ],
        out_specs=pl.BlockSpec((1, chunk, width), lambda bi, i: (bi, i, 0)),
        out_shape=jax.ShapeDtypeStruct((b, s, width), BF16),
        scratch_shapes=[pltpu.VMEM((HEADS, LANES, LANES), F32)],
        compiler_params=_cparams(("parallel", "arbitrary")),
        name="gdn_core",
    )(proj, proj, ab, pad(a_log), pad(dt_bias), norm_g.reshape(1, LANES))


SB_BLK = LANES
SB_TQ = 4096
SB_STATIC_BANDS = 3
F32_EXP_UNDERFLOW = -104.0
NEG_BIG = -1e30


def _sb_bands(qs, kbs, vbs, upper, carries, valid, diag_mask):
    bands, subs = range(len(kbs)), range(len(qs))
    z = [[_dot_nt(qs[i], kbs[d][i]) for i in subs] for d in bands]
    log_sig, log_not = [], []
    for d in bands:
        ls, ln = [], []
        for i in subs:
            zm = jnp.where(diag_mask, z[d][i], NEG_BIG) if (d == 0 and diag_mask is not None) else z[d][i]
            nz = -zm
            nsp = jnp.minimum(nz, 0.0) - jnp.log(1.0 + jnp.exp(jnp.minimum(zm, nz)))
            ls.append(zm + nsp)
            ln.append(nsp)
        log_sig.append(ls)
        log_not.append(ln)
    suffix = [[_dot(log_not[d][i].astype(BF16), upper) for i in subs] for d in bands]
    wgt = [[] for _ in subs]
    carry_out = []
    for i in subs:
        carry = carries[i]
        for d in bands:
            if valid[d][i] is not None:
                carry = jnp.where(valid[d][i], carry, NEG_BIG)
            wgt[i].append(jnp.exp(log_sig[d][i] + suffix[d][i] + carry).astype(BF16))
            carry = carry + (suffix[d][i][:, 0:1] + log_not[d][i][:, 0:1])
        carry_out.append(carry)
    out = [_dot(jnp.concatenate(wgt[i], axis=1), jnp.concatenate([vbs[d][i] for d in bands], axis=0))
           for i in subs]
    return out, carry_out


def _sb_kernel(q_ref, k_ref, v_ref, o_ref, acc_ref, carry_ref):
    gi = pl.program_id(2)
    n_sub = q_ref.shape[1] // SB_BLK
    first_blk = gi * n_sub
    subs = range(n_sub)
    row = lax.broadcasted_iota(jnp.int32, (SB_BLK, SB_BLK), 0)
    col = lax.broadcasted_iota(jnp.int32, (SB_BLK, SB_BLK), 1)
    upper = (row > col).astype(BF16)

    def rows(i):
        return slice(i * SB_BLK, (i + 1) * SB_BLK)

    def bands(d0, n_bands, first):
        qs = [q_ref[0, rows(i), :] for i in subs]
        kbs, vbs, valid = [], [], []
        for d in range(n_bands):
            blk = [first_blk + i - (d0 + d) for i in subs]
            start = [pl.multiple_of(jnp.maximum(blk[i], 0) * SB_BLK, SB_BLK) for i in subs]
            kbs.append([k_ref[0, pl.ds(start[i], SB_BLK), :] for i in subs])
            vbs.append([v_ref[0, pl.ds(start[i], SB_BLK), :] for i in subs])
            valid.append([None if (first and d == 0) else blk[i] >= 0 for i in subs])
        carries = [0.0 if first else carry_ref[:, i:i + 1] for i in subs]
        out, carry_out = _sb_bands(qs, kbs, vbs, upper, carries, valid, col < row if first else None)
        for i in subs:
            if first:
                acc_ref[rows(i), :] = out[i]
            else:
                acc_ref[rows(i), :] += out[i]
            carry_ref[:, i:i + 1] = carry_out[i]

    bands(0, SB_STATIC_BANDS, True)

    def any_row_live():
        return jnp.max(carry_ref[:, 0:n_sub]) >= F32_EXP_UNDERFLOW

    def unfinished(state):
        d, live = state
        return jnp.logical_and(d <= first_blk + n_sub - 1, live)

    def body(state):
        d, _ = state
        bands(d, 1, False)
        return d + 1, any_row_live()

    lax.while_loop(unfinished, body, (jnp.int32(SB_STATIC_BANDS), any_row_live()))
    o_ref[0] = acc_ref[...].astype(o_ref.dtype)


def sb_attention(q, kv, *, tq):
    b, s, width = q.shape
    return pl.pallas_call(
        _sb_kernel,
        grid=(b, HEADS, s // tq),
        in_specs=[
            pl.BlockSpec((1, tq, LANES), lambda bi, h, i: (bi, i, h)),
            pl.BlockSpec((1, s, LANES), lambda bi, h, i: (bi, 0, h)),
            pl.BlockSpec((1, s, LANES), lambda bi, h, i: (bi, 0, HEADS + h)),
        ],
        out_specs=pl.BlockSpec((1, tq, LANES), lambda bi, h, i: (bi, i, h)),
        out_shape=jax.ShapeDtypeStruct((b, s, width), BF16),
        scratch_shapes=[pltpu.VMEM((tq, LANES), F32), pltpu.VMEM((SB_BLK, LANES), F32)],
        compiler_params=_cparams(("parallel", "parallel", "arbitrary")),
        name="sb_attention",
    )(q, kv, kv)


def kernel(x, p, ln_mix, ln_ffn, ln_ple, gdn_w_in, gdn_conv, gdn_a_log, gdn_dt_bias, gdn_norm,
           gdn_w_out, kv_norm, w_kv, k_norm, sb_w_q, sb_q_norm, sb_w_out, ffn_w_in, ffn_w_out,
           ple_w_proj, ple_w_gate):
    b, s, d = x.shape
    depth = p.shape[0]
    n_a = gdn_w_in.shape[0]
    m = b * s
    width = HEADS * LANES

    bf = lambda w: w.astype(BF16)
    gains = lambda g: g.reshape(g.shape[0], 1, g.shape[-1])
    gdn_w_in_b = bf(gdn_w_in[:, :, :4 * width])
    gdn_w_out_b, sb_w_q_b, sb_w_out_b = bf(gdn_w_out), bf(sb_w_q), bf(sb_w_out)
    ffn_w_in_b, ffn_w_out_b, ple_w_proj_b, ple_w_gate_b = bf(ffn_w_in), bf(ffn_w_out), bf(ple_w_proj), bf(ple_w_gate)
    n_ab = gdn_w_in.shape[2] - 4 * width
    gdn_w_ab_b = bf(jnp.pad(gdn_w_in[:, :, 4 * width:], ((0, 0), (0, 0), (0, LANES - n_ab))))
    ln_mix_g, ln_ffn_g, ln_ple_g, sb_q_norm_g = gains(ln_mix), gains(ln_ffn), gains(ln_ple), gains(sb_q_norm)
    p_rows = p.reshape(depth, m, p.shape[-1])

    h = x.reshape(m, d)
    kv = None
    for i in range(depth):
        if i < n_a:
            proj_main, proj_ab = proj_gdn(h, ln_mix_g, gdn_w_in_b, gdn_w_ab_b, gdn_conv, tm=TM, layer=i, seq_len=s)
            o = gdn_core(proj_main.reshape(b, s, 4 * width), proj_ab.reshape(b, s, LANES),
                         gdn_a_log[i], gdn_dt_bias[i], gdn_norm[i])
            w_out, mixer_layer = gdn_w_out_b, i
        else:
            j = i - n_a
            q = proj(h, ln_mix_g, sb_w_q_b, sb_q_norm_g, tm=TM, g_layer=i, w_layer=j, n_normed=width,
                     out_scale=LANES ** -0.5)
            o = sb_attention(q.reshape(b, s, width), kv, tq=SB_TQ)
            w_out, mixer_layer = sb_w_out_b, j
        h = post_mixer(h, o.reshape(m, width), p_rows, w_out, ln_ffn_g, ffn_w_in_b, ffn_w_out_b, ln_ple_g,
                       ple_w_proj_b, ple_w_gate_b, tm=TM, layer=i, mixer_layer=mixer_layer)
        if i == n_a - 1:
            kv = proj(h, kv_norm.reshape(1, d), bf(w_kv), k_norm.reshape(1, LANES), tm=TM,
                      n_normed=width).reshape(b, s, 2 * width)
    return h.reshape(b, s, d)
```

```python
import functools

import jax
import jax.numpy as jnp
from jax import lax
from jax.experimental import pallas as pl
from jax.experimental.pallas import tpu as pltpu

EPS = 1e-6
LANES = 128
HEADS = 8
CONV_W = 4
HALO = 16
VMEM_LIMIT = 56 * 1024 * 1024
TM = 512
FFN_CHUNK = 256

F32 = jnp.float32
BF16 = jnp.bfloat16


def _cparams(sem):
    return pltpu.CompilerParams(dimension_semantics=sem, vmem_limit_bytes=VMEM_LIMIT)


def _dot(a, b):
    return jnp.dot(a, b, preferred_element_type=F32)


def _dot_nt(a, b):
    return lax.dot_general(a, b, (((1,), (1,)), ((), ())), preferred_element_type=F32)


def _rms(x, g):
    return x * lax.rsqrt(jnp.mean(x * x, axis=-1, keepdims=True) + EPS) * g


def _sigmoid(x):
    return 1.0 / (1.0 + jnp.exp(-x))


def _resident(arr, layer=None, cols=None):
    shape = arr.shape if layer is None else arr.shape[1:]
    if cols is not None:
        shape = shape[:-1] + (cols,)
    zeros = (0,) * len(shape)
    if layer is None:
        return pl.BlockSpec(shape, lambda i: zeros, pipeline_mode=pl.Buffered(1))
    return pl.BlockSpec((None,) + shape, lambda i: (layer,) + zeros, pipeline_mode=pl.Buffered(1))


def _proj_kernel(x_ref, g_ref, w_ref, hg_ref, o_ref, *, n_normed, out_scale):
    xn = _rms(x_ref[...], g_ref[...]).astype(BF16)
    hg = hg_ref[...]
    n = w_ref.shape[1]
    step = 2 * LANES
    y_next = _dot(xn, w_ref[:, 0:step])
    for lo in range(0, n, step):
        y = y_next
        if lo + step < n:
            y_next = _dot(xn, w_ref[:, lo + step:lo + 2 * step])
        for c in range(step // LANES):
            col = lo + c * LANES
            yc = y[:, c * LANES:(c + 1) * LANES]
            if col < n_normed:
                yc = _rms(yc, hg) * out_scale
            o_ref[:, col:col + LANES] = yc.astype(o_ref.dtype)


def proj(x, g, w, head_g, *, tm, g_layer=None, w_layer=None, n_normed=0, out_scale=1.0):
    m, d = x.shape
    n = w.shape[-1]
    kern = functools.partial(_proj_kernel, n_normed=n_normed, out_scale=out_scale)
    return pl.pallas_call(
        kern,
        grid=(m // tm,),
        in_specs=[pl.BlockSpec((tm, d), lambda i: (i, 0)), _resident(g, g_layer), _resident(w, w_layer),
                  _resident(head_g, w_layer)],
        out_specs=pl.BlockSpec((tm, n), lambda i: (i, 0)),
        out_shape=jax.ShapeDtypeStruct((m, n), BF16),
        compiler_params=_cparams(("parallel",)),
        name="proj",
    )(x, g, w, head_g)


def _proj_gdn_kernel(x_ref, xh_ref, g_ref, w_ref, wab_ref, conv_ref, o_ref, ab_ref, *, tiles_per_seq):
    i = pl.program_id(0)
    width = HEADS * LANES
    g = g_ref[...]
    xn = _rms(x_ref[...], g).astype(BF16)
    xa = jnp.concatenate([_rms(xh_ref[...], g).astype(BF16), xn], axis=0)
    seq_start = i % tiles_per_seq == 0
    ones = jnp.ones((LANES, LANES), BF16)
    step = 4 * LANES
    chunks = list(range(0, 3 * width, step))
    y_next = _dot(xa, w_ref[:, 0:step])
    for lo in chunks:
        y = jnp.concatenate([jnp.where(seq_start, 0.0, y_next[:HALO]), y_next[HALO:]], axis=0)
        if lo + step < 3 * width:
            y_next = _dot(xa, w_ref[:, lo + step:lo + 2 * step])
        tap = lambda k: conv_ref[CONV_W - 1 - k:CONV_W - k, lo:lo + step]
        y1 = pltpu.roll(y, 1, 0)
        acc = (tap(0) * y + tap(1) * y1 + pltpu.roll(tap(2) * y + tap(3) * y1, 2, 0))[HALO:]
        act = acc * _sigmoid(acc)
        for c in range(step // LANES):
            col = lo + c * LANES
            a = act[:, c * LANES:(c + 1) * LANES]
            if col < 2 * width:
                a = a * lax.rsqrt(_dot((a * a).astype(BF16), ones) + EPS)
                if col < width:
                    a = a * (LANES ** -0.5)
            o_ref[:, col:col + LANES] = a.astype(o_ref.dtype)
    for lo in range(3 * width, 4 * width, step):
        y = _dot(xn, w_ref[:, lo:lo + step])
        o_ref[:, lo:lo + step] = (y * _sigmoid(y)).astype(o_ref.dtype)
    ab_ref[...] = _dot(xn, wab_ref[...])


def proj_gdn(x, g, w_in, w_ab, conv_w, *, tm, layer, seq_len):
    m, d = x.shape
    n = 4 * HEADS * LANES
    halo_blocks = tm // HALO
    kern = functools.partial(_proj_gdn_kernel, tiles_per_seq=seq_len // tm)
    return pl.pallas_call(
        kern,
        grid=(m // tm,),
        in_specs=[
            pl.BlockSpec((tm, d), lambda i: (i, 0)),
            pl.BlockSpec((HALO, d), lambda i: (jnp.maximum(i * halo_blocks - 1, 0), 0)),
            _resident(g, layer), _resident(w_in, layer, cols=n), _resident(w_ab, layer),
            _resident(conv_w, layer),
        ],
        out_specs=[pl.BlockSpec((tm, n), lambda i: (i, 0)), pl.BlockSpec((tm, LANES), lambda i: (i, 0))],
        out_shape=[jax.ShapeDtypeStruct((m, n), BF16), jax.ShapeDtypeStruct((m, LANES), F32)],
        compiler_params=_cparams(("parallel",)),
        name="proj_gdn",
    )(x, x, g, w_in, w_ab, conv_w)


def _post_kernel(h_ref, o_ref, p_ref, wo_ref, gf_ref, w1_ref, w2_ref, gp_ref, wp_ref, wg_ref, out_ref,
                 hn_ref, acc_ref):
    hidden = w2_ref.shape[0]
    h1 = h_ref[...] + _dot(o_ref[...], wo_ref[...])
    hn_ref[...] = _rms(h1, gf_ref[...]).astype(BF16)
    acc_ref[...] = h1

    def gate_and_linear(lo):
        hn = hn_ref[...]
        return (_dot(hn, w1_ref[:, lo:lo + FFN_CHUNK]),
                _dot(hn, w1_ref[:, hidden + lo:hidden + lo + FFN_CHUNK]))

    a, b = gate_and_linear(0)
    for lo in range(0, hidden, FFN_CHUNK):
        act = (a * _sigmoid(a) * b).astype(BF16)
        if lo + FFN_CHUNK < hidden:
            a, b = gate_and_linear(lo + FFN_CHUNK)
        acc_ref[...] += _dot(act, w2_ref[lo:lo + FFN_CHUNK, :])
    h2 = acc_ref[...]
    gate = _sigmoid(_dot(_rms(h2, gp_ref[...]).astype(BF16), wg_ref[...]))
    emb = _dot(p_ref[...].astype(BF16), wp_ref[...])
    out_ref[...] = h2 + emb * gate


def post_mixer(h, o, p, w_out, g_ffn, w1, w2, g_ple, wp, wg, *, tm, layer, mixer_layer):
    m, d = h.shape
    return pl.pallas_call(
        _post_kernel,
        grid=(m // tm,),
        in_specs=[
            pl.BlockSpec((tm, d), lambda i: (i, 0)),
            pl.BlockSpec((tm, o.shape[1]), lambda i: (i, 0)),
            pl.BlockSpec((None, tm, p.shape[2]), lambda i: (layer, i, 0)),
            _resident(w_out, mixer_layer), _resident(g_ffn, layer), _resident(w1, layer), _resident(w2, layer),
            _resident(g_ple, layer), _resident(wp, layer), _resident(wg, layer),
        ],
        out_specs=pl.BlockSpec((tm, d), lambda i: (i, 0)),
        out_shape=jax.ShapeDtypeStruct((m, d), F32),
        scratch_shapes=[pltpu.VMEM((tm, d), BF16), pltpu.VMEM((tm, d), F32)],
        compiler_params=_cparams(("parallel",)),
        name="post_mixer",
    )(h, o, p, w_out, g_ffn, w1, w2, g_ple, wp, wg)


TRI_BASE_LOG2 = 4
GDN_CHUNK = LANES
GDN_CHUNKS_PER_STEP = 8


def _tri_masks(row, col):
    c = row.shape[0]
    eye = (row == col).astype(F32)
    diag = (row >> TRI_BASE_LOG2) == (col >> TRI_BASE_LOG2)
    levels = []
    sh = TRI_BASE_LOG2
    while (1 << sh) < c:
        rb = row >> sh
        levels.append(((rb & 1) == 1) & ((col >> sh) == rb - 1))
        sh += 1
    return eye, diag, levels


def _bdot(a, b):
    return _dot(a.astype(BF16), b.astype(BF16))


def _split_bf16(x):
    hi = x.astype(BF16)
    return hi, (x - hi.astype(F32)).astype(BF16)


def _gdn_kernel(qkv_ref, gate_ref, ab_ref, alog_ref, dtb_ref, ng_ref, o_ref, state_ref):
    i = pl.program_id(1)
    c = GDN_CHUNK
    n_ck = qkv_ref.shape[1] // c
    width = HEADS * LANES
    heads = range(HEADS)
    items = [(ck, h) for ck in range(n_ck) for h in heads]
    n = range(len(items))

    @pl.when(i == 0)
    def _():
        state_ref[...] = jnp.zeros_like(state_ref)

    row = lax.broadcasted_iota(jnp.int32, (c, c), 0)
    col = lax.broadcasted_iota(jnp.int32, (c, c), 1)
    causal = row >= col
    strict = row > col
    eye, diag, levels = _tri_masks(row, col)
    tri_incl = causal.astype(BF16)

    gc_all, gc_t, beta_all, eg_all, egl_all, gl_all = [], [], [], [], [], []
    for ck in range(n_ck):
        ab = ab_ref[0, ck * c:(ck + 1) * c, :]
        x_dt = ab + dtb_ref[...]
        softplus = jnp.maximum(x_dt, 0.0) + jnp.log(1.0 + jnp.exp(-jnp.abs(x_dt)))
        g_hi, g_lo = _split_bf16(-jnp.exp(alog_ref[...]) * softplus)
        gc = _dot(tri_incl, g_hi) + _dot(tri_incl, g_lo)
        g_last = gc[c - 1:c, :]
        gc_all.append(gc)
        gc_t.append(gc.T)
        beta_all.append(_sigmoid(ab))
        eg_all.append(jnp.exp(gc))
        egl_all.append(jnp.exp(g_last - gc))
        gl_all.append(jnp.exp(g_last))

    def section(sec, ck, h):
        lo = sec * width + h * LANES
        return qkv_ref[0, ck * c:(ck + 1) * c, lo:lo + LANES]

    qb = [section(0, ck, h) for ck, h in items]
    kb = [section(1, ck, h) for ck, h in items]
    q = [qb[j].astype(F32) for j in n]
    k = [kb[j].astype(F32) for j in n]
    v = [section(2, ck, h).astype(F32) for ck, h in items]
    decay = []
    for ck, h in items:
        diff = jnp.where(causal, gc_all[ck][:, h:h + 1] - gc_t[ck][h:h + 1, :], 0.0)
        decay.append(jnp.where(causal, jnp.exp(diff), 0.0))
    beta = [beta_all[ck][:, HEADS + h:HEADS + h + 1] for ck, h in items]
    eg = [eg_all[ck][:, h:h + 1] for ck, h in items]

    qkk = [_dot_nt(jnp.concatenate([qb[j], kb[j]], axis=0), kb[j]) for j in n]
    qk = [(qkk[j][:c] * decay[j]).astype(BF16) for j in n]
    a_low = [jnp.where(strict, beta[j] * qkk[j][c:] * decay[j], 0.0) for j in n]

    pw = [jnp.where(diag, -a_low[j], 0.0) for j in n]
    t = [eye + pw[j] for j in n]
    for _ in range(TRI_BASE_LOG2 - 1):
        pw = [_bdot(pw[j], pw[j]) for j in n]
        t = [t[j] + _bdot(t[j], pw[j]) for j in n]
    for lvl in levels:
        y = [_bdot(jnp.where(lvl, a_low[j], 0.0), t[j]) for j in n]
        t = [t[j] - _bdot(t[j], y[j]) for j in n]

    rhs = [jnp.concatenate([v[j] * beta[j], k[j] * (beta[j] * eg[j])], axis=1) for j in n]
    sol = [_bdot(t[j], rhs[j]) for j in n]
    lhs1 = [jnp.concatenate([sol[j][:, LANES:], q[j] * eg[j]], axis=0).astype(BF16) for j in n]
    lhs2 = [jnp.concatenate([qk[j], (k[j] * egl_all[ck][:, h:h + 1]).T.astype(BF16)], axis=0)
            for j, (ck, h) in enumerate(items)]

    ng = ng_ref[...]
    state = [state_ref[h] for h in heads]
    for ck in range(n_ck):
        js = [ck * HEADS + h for h in heads]
        sb = [state[h].astype(BF16) for h in heads]
        r1 = [_dot(lhs1[js[h]], sb[h]) for h in heads]
        vb = [(sol[js[h]][:, :LANES] - r1[h][:c]).astype(BF16) for h in heads]
        r2 = [_dot(lhs2[js[h]], vb[h]) for h in heads]
        state = [state[h] * gl_all[ck][:, h:h + 1] + r2[h][c:] for h in heads]
        for h in heads:
            o = r1[h][c:] + r2[h][:c]
            gate = gate_ref[0, ck * c:(ck + 1) * c, h * LANES:(h + 1) * LANES].astype(F32)
            scale = lax.rsqrt(jnp.mean(o * o, axis=-1, keepdims=True) + EPS)
            o_ref[0, ck * c:(ck + 1) * c, h * LANES:(h + 1) * LANES] = (o * scale * ng * gate).astype(o_ref.dtype)
    for h in heads:
        state_ref[h] = state[h]


def gdn_core(proj, ab, a_log, dt_bias, norm_g):
    b, s, _ = proj.shape
    width = HEADS * LANES
    chunk = GDN_CHUNK * GDN_CHUNKS_PER_STEP
    pad = lambda v: jnp.zeros((1, LANES), F32).at[0, :HEADS].set(v)
    return pl.pallas_call(
        _gdn_kernel,
        grid=(b, s // chunk),
        in_specs=[
            pl.BlockSpec((1, chunk, 3 * width), lambda bi, i: (bi, i, 0)),
            pl.BlockSpec((1, chunk, width), lambda bi, i: (bi, i, 3)),
            pl.BlockSpec((1, chunk, LANES), lambda bi, i: (bi, i, 0)),
            pl.BlockSpec((1, LANES), lambda bi, i: (0, 0)),
            pl.BlockSpec((1, LANES), lambda bi, i: (0, 0)),
            pl.BlockSpec((1, LANES), lambda bi, i: (0, 0)),
        ],
        out_specs=pl.BlockSpec((1, chunk, width), lambda bi, i: (bi, i, 0)),
        out_shape=jax.ShapeDtypeStruct((b, s, width), BF16),
        scratch_shapes=[pltpu.VMEM((HEADS, LANES, LANES), F32)],
        compiler_params=_cparams(("parallel", "arbitrary")),
        name="gdn_core",
    )(proj, proj, ab, pad(a_log), pad(dt_bias), norm_g.reshape(1, LANES))


SB_BLK = LANES
SB_TQ = 4096
SB_STATIC_BANDS = 3
F32_EXP_UNDERFLOW = -104.0
NEG_BIG = -1e30


def _sb_bands(qs, kbs, vbs, upper, carries, valid, diag_mask):
    bands, subs = range(len(kbs)), range(len(qs))
    z = [[_dot_nt(qs[i], kbs[d][i]) for i in subs] for d in bands]
    log_sig, log_not = [], []
    for d in bands:
        ls, ln = [], []
        for i in subs:
            zm = jnp.where(diag_mask, z[d][i], NEG_BIG) if (d == 0 and diag_mask is not None) else z[d][i]
            nz = -zm
            nsp = jnp.minimum(nz, 0.0) - jnp.log(1.0 + jnp.exp(jnp.minimum(zm, nz)))
            ls.append(zm + nsp)
            ln.append(nsp)
        log_sig.append(ls)
        log_not.append(ln)
    suffix = [[_dot(log_not[d][i].astype(BF16), upper) for i in subs] for d in bands]
    wgt = [[] for _ in subs]
    carry_out = []
    for i in subs:
        carry = carries[i]
        for d in bands:
            if valid[d][i] is not None:
                carry = jnp.where(valid[d][i], carry, NEG_BIG)
            wgt[i].append(jnp.exp(log_sig[d][i] + suffix[d][i] + carry).astype(BF16))
            carry = carry + (suffix[d][i][:, 0:1] + log_not[d][i][:, 0:1])
        carry_out.append(carry)
    out = [_dot(jnp.concatenate(wgt[i], axis=1), jnp.concatenate([vbs[d][i] for d in bands], axis=0))
           for i in subs]
    return out, carry_out


def _sb_kernel(q_ref, k_ref, v_ref, o_ref, acc_ref, carry_ref):
    gi = pl.program_id(2)
    n_sub = q_ref.shape[1] // SB_BLK
    first_blk = gi * n_sub
    subs = range(n_sub)
    row = lax.broadcasted_iota(jnp.int32, (SB_BLK, SB_BLK), 0)
    col = lax.broadcasted_iota(jnp.int32, (SB_BLK, SB_BLK), 1)
    upper = (row > col).astype(BF16)

    def rows(i):
        return slice(i * SB_BLK, (i + 1) * SB_BLK)

    def bands(d0, n_bands, first):
        qs = [q_ref[0, rows(i), :] for i in subs]
        kbs, vbs, valid = [], [], []
        for d in range(n_bands):
            blk = [first_blk + i - (d0 + d) for i in subs]
            start = [pl.multiple_of(jnp.maximum(blk[i], 0) * SB_BLK, SB_BLK) for i in subs]
            kbs.append([k_ref[0, pl.ds(start[i], SB_BLK), :] for i in subs])
            vbs.append([v_ref[0, pl.ds(start[i], SB_BLK), :] for i in subs])
            valid.append([None if (first and d == 0) else blk[i] >= 0 for i in subs])
        carries = [0.0 if first else carry_ref[:, i:i + 1] for i in subs]
        out, carry_out = _sb_bands(qs, kbs, vbs, upper, carries, valid, col < row if first else None)
        for i in subs:
            if first:
                acc_ref[rows(i), :] = out[i]
            else:
                acc_ref[rows(i), :] += out[i]
            carry_ref[:, i:i + 1] = carry_out[i]

    bands(0, SB_STATIC_BANDS, True)

    def any_row_live():
        return jnp.max(carry_ref[:, 0:n_sub]) >= F32_EXP_UNDERFLOW

    def unfinished(state):
        d, live = state
        return jnp.logical_and(d <= first_blk + n_sub - 1, live)

    def body(state):
        d, _ = state
        bands(d, 1, False)
        return d + 1, any_row_live()

    lax.while_loop(unfinished, body, (jnp.int32(SB_STATIC_BANDS), any_row_live()))
    o_ref[0] = acc_ref[...].astype(o_ref.dtype)


def sb_attention(q, kv, *, tq):
    b, s, width = q.shape
    return pl.pallas_call(
        _sb_kernel,
        grid=(b, HEADS, s // tq),
        in_specs=[
            pl.BlockSpec((1, tq, LANES), lambda bi, h, i: (bi, i, h)),
            pl.BlockSpec((1, s, LANES), lambda bi, h, i: (bi, 0, h)),
            pl.BlockSpec((1, s, LANES), lambda bi, h, i: (bi, 0, HEADS + h)),
        ],
        out_specs=pl.BlockSpec((1, tq, LANES), lambda bi, h, i: (bi, i, h)),
        out_shape=jax.ShapeDtypeStruct((b, s, width), BF16),
        scratch_shapes=[pltpu.VMEM((tq, LANES), F32), pltpu.VMEM((SB_BLK, LANES), F32)],
        compiler_params=_cparams(("parallel", "parallel", "arbitrary")),
        name="sb_attention",
    )(q, kv, kv)


def kernel(x, p, ln_mix, ln_ffn, ln_ple, gdn_w_in, gdn_conv, gdn_a_log, gdn_dt_bias, gdn_norm,
           gdn_w_out, kv_norm, w_kv, k_norm, sb_w_q, sb_q_norm, sb_w_out, ffn_w_in, ffn_w_out,
           ple_w_proj, ple_w_gate):
    b, s, d = x.shape
    depth = p.shape[0]
    n_a = gdn_w_in.shape[0]
    m = b * s
    width = HEADS * LANES

    bf = lambda w: w.astype(BF16)
    gains = lambda g: g.reshape(g.shape[0], 1, g.shape[-1])
    gdn_w_in_b, gdn_w_out_b, sb_w_q_b, sb_w_out_b = bf(gdn_w_in), bf(gdn_w_out), bf(sb_w_q), bf(sb_w_out)
    ffn_w_in_b, ffn_w_out_b, ple_w_proj_b, ple_w_gate_b = bf(ffn_w_in), bf(ffn_w_out), bf(ple_w_proj), bf(ple_w_gate)
    n_ab = gdn_w_in.shape[2] - 4 * width
    gdn_w_ab_b = bf(jnp.pad(gdn_w_in[:, :, 4 * width:], ((0, 0), (0, 0), (0, LANES - n_ab))))
    ln_mix_g, ln_ffn_g, ln_ple_g, sb_q_norm_g = gains(ln_mix), gains(ln_ffn), gains(ln_ple), gains(sb_q_norm)
    p_rows = p.reshape(depth, m, p.shape[-1])

    h = x.reshape(m, d)
    kv = None
    for i in range(depth):
        if i < n_a:
            proj_main, proj_ab = proj_gdn(h, ln_mix_g, gdn_w_in_b, gdn_w_ab_b, gdn_conv, tm=TM, layer=i, seq_len=s)
            o = gdn_core(proj_main.reshape(b, s, 4 * width), proj_ab.reshape(b, s, LANES),
                         gdn_a_log[i], gdn_dt_bias[i], gdn_norm[i])
            w_out, mixer_layer = gdn_w_out_b, i
        else:
            j = i - n_a
            q = proj(h, ln_mix_g, sb_w_q_b, sb_q_norm_g, tm=TM, g_layer=i, w_layer=j, n_normed=width,
                     out_scale=LANES ** -0.5)
            o = sb_attention(q.reshape(b, s, width), kv, tq=SB_TQ)
            w_out, mixer_layer = sb_w_out_b, j
        h = post_mixer(h, o.reshape(m, width), p_rows, w_out, ln_ffn_g, ffn_w_in_b, ffn_w_out_b, ln_ple_g,
                       ple_w_proj_b, ple_w_gate_b, tm=TM, layer=i, mixer_layer=mixer_layer)
        if i == n_a - 1:
            kv = proj(h, kv_norm.reshape(1, d), bf(w_kv), k_norm.reshape(1, LANES), tm=TM,
                      n_normed=width).reshape(b, s, 2 * width)
    return h.reshape(b, s, d)
```

```python
import functools

import jax
import jax.numpy as jnp
from jax import lax
from jax.experimental import pallas as pl
from jax.experimental.pallas import tpu as pltpu

EPS = 1e-6
LANES = 128
HEADS = 8
CONV_W = 4
HALO = 16
VMEM_LIMIT = 56 * 1024 * 1024
TM = 512
FFN_CHUNK = 256

F32 = jnp.float32
BF16 = jnp.bfloat16


def _cparams(sem):
    return pltpu.CompilerParams(dimension_semantics=sem, vmem_limit_bytes=VMEM_LIMIT)


def _dot(a, b):
    return jnp.dot(a, b, preferred_element_type=F32)


def _dot_nt(a, b):
    return lax.dot_general(a, b, (((1,), (1,)), ((), ())), preferred_element_type=F32)


def _rms(x, g):
    return x * lax.rsqrt(jnp.mean(x * x, axis=-1, keepdims=True) + EPS) * g


def _sigmoid(x):
    return 1.0 / (1.0 + jnp.exp(-x))


def _resident(arr, layer=None, cols=None):
    shape = arr.shape if layer is None else arr.shape[1:]
    if cols is not None:
        shape = shape[:-1] + (cols,)
    zeros = (0,) * len(shape)
    if layer is None:
        return pl.BlockSpec(shape, lambda i: zeros, pipeline_mode=pl.Buffered(1))
    return pl.BlockSpec((None,) + shape, lambda i: (layer,) + zeros, pipeline_mode=pl.Buffered(1))


def _proj_kernel(x_ref, g_ref, w_ref, hg_ref, o_ref, *, n_normed, out_scale):
    xn = _rms(x_ref[...], g_ref[...]).astype(BF16)
    hg = hg_ref[...]
    n = w_ref.shape[1]
    step = 2 * LANES
    y_next = _dot(xn, w_ref[:, 0:step])
    for lo in range(0, n, step):
        y = y_next
        if lo + step < n:
            y_next = _dot(xn, w_ref[:, lo + step:lo + 2 * step])
        for c in range(step // LANES):
            col = lo + c * LANES
            yc = y[:, c * LANES:(c + 1) * LANES]
            if col < n_normed:
                yc = _rms(yc, hg) * out_scale
            o_ref[:, col:col + LANES] = yc.astype(o_ref.dtype)


def proj(x, g, w, head_g, *, tm, g_layer=None, w_layer=None, n_normed=0, out_scale=1.0):
    m, d = x.shape
    n = w.shape[-1]
    kern = functools.partial(_proj_kernel, n_normed=n_normed, out_scale=out_scale)
    return pl.pallas_call(
        kern,
        grid=(m // tm,),
        in_specs=[pl.BlockSpec((tm, d), lambda i: (i, 0)), _resident(g, g_layer), _resident(w, w_layer),
                  _resident(head_g, w_layer)],
        out_specs=pl.BlockSpec((tm, n), lambda i: (i, 0)),
        out_shape=jax.ShapeDtypeStruct((m, n), BF16),
        compiler_params=_cparams(("parallel",)),
        name="proj",
    )(x, g, w, head_g)


def _proj_gdn_kernel(x_ref, xh_ref, g_ref, w_ref, wab_ref, conv_ref, o_ref, ab_ref, *, tiles_per_seq):
    i = pl.program_id(0)
    width = HEADS * LANES
    g = g_ref[...]
    xn = _rms(x_ref[...], g).astype(BF16)
    xa = jnp.concatenate([_rms(xh_ref[...], g).astype(BF16), xn], axis=0)
    seq_start = i % tiles_per_seq == 0
    ones = jnp.ones((LANES, LANES), BF16)
    step = 4 * LANES
    chunks = list(range(0, 3 * width, step))
    y_next = _dot(xa, w_ref[:, 0:step])
    for lo in chunks:
        y = jnp.concatenate([jnp.where(seq_start, 0.0, y_next[:HALO]), y_next[HALO:]], axis=0)
        if lo + step < 3 * width:
            y_next = _dot(xa, w_ref[:, lo + step:lo + 2 * step])
        tap = lambda k: conv_ref[CONV_W - 1 - k:CONV_W - k, lo:lo + step]
        y1 = pltpu.roll(y, 1, 0)
        acc = (tap(0) * y + tap(1) * y1 + pltpu.roll(tap(2) * y + tap(3) * y1, 2, 0))[HALO:]
        act = acc * _sigmoid(acc)
        for c in range(step // LANES):
            col = lo + c * LANES
            a = act[:, c * LANES:(c + 1) * LANES]
            if col < 2 * width:
                a = a * lax.rsqrt(_dot((a * a).astype(BF16), ones) + EPS)
                if col < width:
                    a = a * (LANES ** -0.5)
            o_ref[:, col:col + LANES] = a.astype(o_ref.dtype)
    for lo in range(3 * width, 4 * width, step):
        y = _dot(xn, w_ref[:, lo:lo + step])
        o_ref[:, lo:lo + step] = (y * _sigmoid(y)).astype(o_ref.dtype)
    ab_ref[...] = _dot(xn, wab_ref[...])


def proj_gdn(x, g, w_in, w_ab, conv_w, *, tm, layer, seq_len):
    m, d = x.shape
    n = 4 * HEADS * LANES
    halo_blocks = tm // HALO
    kern = functools.partial(_proj_gdn_kernel, tiles_per_seq=seq_len // tm)
    return pl.pallas_call(
        kern,
        grid=(m // tm,),
        in_specs=[
            pl.BlockSpec((tm, d), lambda i: (i, 0)),
            pl.BlockSpec((HALO, d), lambda i: (jnp.maximum(i * halo_blocks - 1, 0), 0)),
            _resident(g, layer), _resident(w_in, layer, cols=n), _resident(w_ab, layer),
            _resident(conv_w, layer),
        ],
        out_specs=[pl.BlockSpec((tm, n), lambda i: (i, 0)), pl.BlockSpec((tm, LANES), lambda i: (i, 0))],
        out_shape=[jax.ShapeDtypeStruct((m, n), BF16), jax.ShapeDtypeStruct((m, LANES), F32)],
        compiler_params=_cparams(("parallel",)),
        name="proj_gdn",
    )(x, x, g, w_in, w_ab, conv_w)


def _post_kernel(h_ref, o_ref, p_ref, wo_ref, gf_ref, w1_ref, w2_ref, gp_ref, wp_ref, wg_ref, out_ref,
                 hn_ref, acc_ref):
    hidden = w2_ref.shape[0]
    h1 = h_ref[...] + _dot(o_ref[...], wo_ref[...])
    hn_ref[...] = _rms(h1, gf_ref[...]).astype(BF16)
    acc_ref[...] = h1

    def gate_and_linear(lo):
        hn = hn_ref[...]
        return (_dot(hn, w1_ref[:, lo:lo + FFN_CHUNK]),
                _dot(hn, w1_ref[:, hidden + lo:hidden + lo + FFN_CHUNK]))

    a, b = gate_and_linear(0)
    for lo in range(0, hidden, FFN_CHUNK):
        act = (a * _sigmoid(a) * b).astype(BF16)
        if lo + FFN_CHUNK < hidden:
            a, b = gate_and_linear(lo + FFN_CHUNK)
        acc_ref[...] += _dot(act, w2_ref[lo:lo + FFN_CHUNK, :])
    h2 = acc_ref[...]
    gate = _sigmoid(_dot(_rms(h2, gp_ref[...]).astype(BF16), wg_ref[...]))
    emb = _dot(p_ref[...].astype(BF16), wp_ref[...])
    out_ref[...] = h2 + emb * gate


def post_mixer(h, o, p, w_out, g_ffn, w1, w2, g_ple, wp, wg, *, tm, layer, mixer_layer):
    m, d = h.shape
    return pl.pallas_call(
        _post_kernel,
        grid=(m // tm,),
        in_specs=[
            pl.BlockSpec((tm, d), lambda i: (i, 0)),
            pl.BlockSpec((tm, o.shape[1]), lambda i: (i, 0)),
            pl.BlockSpec((None, tm, p.shape[2]), lambda i: (layer, i, 0)),
            _resident(w_out, mixer_layer), _resident(g_ffn, layer), _resident(w1, layer), _resident(w2, layer),
            _resident(g_ple, layer), _resident(wp, layer), _resident(wg, layer),
        ],
        out_specs=pl.BlockSpec((tm, d), lambda i: (i, 0)),
        out_shape=jax.ShapeDtypeStruct((m, d), F32),
        scratch_shapes=[pltpu.VMEM((tm, d), BF16), pltpu.VMEM((tm, d), F32)],
        compiler_params=_cparams(("parallel",)),
        name="post_mixer",
    )(h, o, p, w_out, g_ffn, w1, w2, g_ple, wp, wg)


TRI_BASE_LOG2 = 4
GDN_CHUNK = LANES
GDN_CHUNKS_PER_STEP = 8


def _tri_masks(row, col):
    c = row.shape[0]
    eye = (row == col).astype(F32)
    diag = (row >> TRI_BASE_LOG2) == (col >> TRI_BASE_LOG2)
    levels = []
    sh = TRI_BASE_LOG2
    while (1 << sh) < c:
        rb = row >> sh
        levels.append(((rb & 1) == 1) & ((col >> sh) == rb - 1))
        sh += 1
    return eye, diag, levels


def _bdot(a, b):
    return _dot(a.astype(BF16), b.astype(BF16))


def _split_bf16(x):
    hi = x.astype(BF16)
    return hi, (x - hi.astype(F32)).astype(BF16)


def _gdn_kernel(qkv_ref, gate_ref, ab_ref, alog_ref, dtb_ref, ng_ref, o_ref, state_ref):
    i = pl.program_id(1)
    c = GDN_CHUNK
    n_ck = qkv_ref.shape[1] // c
    width = HEADS * LANES
    heads = range(HEADS)
    items = [(ck, h) for ck in range(n_ck) for h in heads]
    n = range(len(items))

    @pl.when(i == 0)
    def _():
        state_ref[...] = jnp.zeros_like(state_ref)

    row = lax.broadcasted_iota(jnp.int32, (c, c), 0)
    col = lax.broadcasted_iota(jnp.int32, (c, c), 1)
    causal = row >= col
    strict = row > col
    eye, diag, levels = _tri_masks(row, col)
    tri_incl = causal.astype(BF16)

    gc_all, gc_t, beta_all, eg_all, egl_all, gl_all = [], [], [], [], [], []
    for ck in range(n_ck):
        ab = ab_ref[0, ck * c:(ck + 1) * c, :]
        x_dt = ab + dtb_ref[...]
        softplus = jnp.maximum(x_dt, 0.0) + jnp.log(1.0 + jnp.exp(-jnp.abs(x_dt)))
        g_hi, g_lo = _split_bf16(-jnp.exp(alog_ref[...]) * softplus)
        gc = _dot(tri_incl, g_hi) + _dot(tri_incl, g_lo)
        g_last = gc[c - 1:c, :]
        gc_all.append(gc)
        gc_t.append(gc.T)
        beta_all.append(_sigmoid(ab))
        eg_all.append(jnp.exp(gc))
        egl_all.append(jnp.exp(g_last - gc))
        gl_all.append(jnp.exp(g_last))

    def section(sec, ck, h):
        lo = sec * width + h * LANES
        return qkv_ref[0, ck * c:(ck + 1) * c, lo:lo + LANES]

    qb = [section(0, ck, h) for ck, h in items]
    kb = [section(1, ck, h) for ck, h in items]
    q = [qb[j].astype(F32) for j in n]
    k = [kb[j].astype(F32) for j in n]
    v = [section(2, ck, h).astype(F32) for ck, h in items]
    decay = []
    for ck, h in items:
        diff = jnp.where(causal, gc_all[ck][:, h:h + 1] - gc_t[ck][h:h + 1, :], 0.0)
        decay.append(jnp.where(causal, jnp.exp(diff), 0.0))
    beta = [beta_all[ck][:, HEADS + h:HEADS + h + 1] for ck, h in items]
    eg = [eg_all[ck][:, h:h + 1] for ck, h in items]

    qkk = [_dot_nt(jnp.concatenate([qb[j], kb[j]], axis=0), kb[j]) for j in n]
    qk = [(qkk[j][:c] * decay[j]).astype(BF16) for j in n]
    a_low = [jnp.where(strict, beta[j] * qkk[j][c:] * decay[j], 0.0) for j in n]

    pw = [jnp.where(diag, -a_low[j], 0.0) for j in n]
    t = [eye + pw[j] for j in n]
    for _ in range(TRI_BASE_LOG2 - 1):
        pw = [_bdot(pw[j], pw[j]) for j in n]
        t = [t[j] + _bdot(t[j], pw[j]) for j in n]
    for lvl in levels:
        y = [_bdot(jnp.where(lvl, a_low[j], 0.0), t[j]) for j in n]
        t = [t[j] - _bdot(t[j], y[j]) for j in n]

    rhs = [jnp.concatenate([v[j] * beta[j], k[j] * (beta[j] * eg[j])], axis=1) for j in n]
    sol = [_bdot(t[j], rhs[j]) for j in n]
    lhs1 = [jnp.concatenate([sol[j][:, LANES:], q[j] * eg[j]], axis=0).astype(BF16) for j in n]
    lhs2 = [jnp.concatenate([qk[j], (k[j] * egl_all[ck][:, h:h + 1]).T.astype(BF16)], axis=0)
            for j, (ck, h) in enumerate(items)]

    ng = ng_ref[...]
    state = [state_ref[h] for h in heads]
    for ck in range(n_ck):
        js = [ck * HEADS + h for h in heads]
        sb = [state[h].astype(BF16) for h in heads]
        r1 = [_dot(lhs1[js[h]], sb[h]) for h in heads]
        vb = [(sol[js[h]][:, :LANES] - r1[h][:c]).astype(BF16) for h in heads]
        r2 = [_dot(lhs2[js[h]], vb[h]) for h in heads]
        state = [state[h] * gl_all[ck][:, h:h + 1] + r2[h][c:] for h in heads]
        for h in heads:
            o = r1[h][c:] + r2[h][:c]
            gate = gate_ref[0, ck * c:(ck + 1) * c, h * LANES:(h + 1) * LANES].astype(F32)
            scale = lax.rsqrt(jnp.mean(o * o, axis=-1, keepdims=True) + EPS)
            o_ref[0, ck * c:(ck + 1) * c, h * LANES:(h + 1) * LANES] = (o * scale * ng * gate).astype(o_ref.dtype)
    for h in heads:
        state_ref[h] = state[h]


def gdn_core(proj, ab, a_log, dt_bias, norm_g):
    b, s, _ = proj.shape
    width = HEADS * LANES
    chunk = GDN_CHUNK * GDN_CHUNKS_PER_STEP
    pad = lambda v: jnp.zeros((1, LANES), F32).at[0, :HEADS].set(v)
    return pl.pallas_call(
        _gdn_kernel,
        grid=(b, s // chunk),
        in_specs=[
            pl.BlockSpec((1, chunk, 3 * width), lambda bi, i: (bi, i, 0)),
            pl.BlockSpec((1, chunk, width), lambda bi, i: (bi, i, 3)),
            pl.BlockSpec((1, chunk, LANES), lambda bi, i: (bi, i, 0)),
            pl.BlockSpec((1, LANES), lambda bi, i: (0, 0)),
            pl.BlockSpec((1, LANES), lambda bi, i: (0, 0)),
            pl.BlockSpec((1, LANES), lambda bi, i: (0, 0)),
        ],
        out_specs=pl.BlockSpec((1, chunk, width), lambda bi, i: (bi, i, 0)),
        out_shape=jax.ShapeDtypeStruct((b, s, width), BF16),
        scratch_shapes=[pltpu.VMEM((HEADS, LANES, LANES), F32)],
        compiler_params=_cparams(("parallel", "arbitrary")),
        name="gdn_core",
    )(proj, proj, ab, pad(a_log), pad(dt_bias), norm_g.reshape(1, LANES))


SB_BLK = LANES
SB_TQ = 4096
SB_STATIC_BANDS = 3
F32_EXP_UNDERFLOW = -104.0
NEG_BIG = -1e30


def _sb_bands(qs, kbs, vbs, upper, carries, valid, diag_mask):
    bands, subs = range(len(kbs)), range(len(qs))
    z = [[_dot_nt(qs[i], kbs[d][i]) for i in subs] for d in bands]
    log_sig, log_not = [], []
    for d in bands:
        ls, ln = [], []
        for i in subs:
            zm = jnp.where(diag_mask, z[d][i], NEG_BIG) if (d == 0 and diag_mask is not None) else z[d][i]
            nz = -zm
            nsp = jnp.minimum(nz, 0.0) - jnp.log(1.0 + jnp.exp(jnp.minimum(zm, nz)))
            ls.append(zm + nsp)
            ln.append(nsp)
        log_sig.append(ls)
        log_not.append(ln)
    suffix = [[_dot(log_not[d][i].astype(BF16), upper) for i in subs] for d in bands]
    wgt = [[] for _ in subs]
    carry_out = []
    for i in subs:
        carry = carries[i]
        for d in bands:
            if valid[d][i] is not None:
                carry = jnp.where(valid[d][i], carry, NEG_BIG)
            wgt[i].append(jnp.exp(log_sig[d][i] + suffix[d][i] + carry).astype(BF16))
            carry = carry + (suffix[d][i][:, 0:1] + log_not[d][i][:, 0:1])
        carry_out.append(carry)
    out = [_dot(jnp.concatenate(wgt[i], axis=1), jnp.concatenate([vbs[d][i] for d in bands], axis=0))
           for i in subs]
    return out, carry_out


def _sb_kernel(q_ref, k_ref, v_ref, o_ref, acc_ref, carry_ref):
    gi = pl.program_id(2)
    n_sub = q_ref.shape[1] // SB_BLK
    first_blk = gi * n_sub
    subs = range(n_sub)
    row = lax.broadcasted_iota(jnp.int32, (SB_BLK, SB_BLK), 0)
    col = lax.broadcasted_iota(jnp.int32, (SB_BLK, SB_BLK), 1)
    upper = (row > col).astype(BF16)

    def rows(i):
        return slice(i * SB_BLK, (i + 1) * SB_BLK)

    def bands(d0, n_bands, first):
        qs = [q_ref[0, rows(i), :] for i in subs]
        kbs, vbs, valid = [], [], []
        for d in range(n_bands):
            blk = [first_blk + i - (d0 + d) for i in subs]
            start = [pl.multiple_of(jnp.maximum(blk[i], 0) * SB_BLK, SB_BLK) for i in subs]
            kbs.append([k_ref[0, pl.ds(start[i], SB_BLK), :] for i in subs])
            vbs.append([v_ref[0, pl.ds(start[i], SB_BLK), :] for i in subs])
            valid.append([None if (first and d == 0) else blk[i] >= 0 for i in subs])
        carries = [0.0 if first else carry_ref[:, i:i + 1] for i in subs]
        out, carry_out = _sb_bands(qs, kbs, vbs, upper, carries, valid, col < row if first else None)
        for i in subs:
            if first:
                acc_ref[rows(i), :] = out[i]
            else:
                acc_ref[rows(i), :] += out[i]
            carry_ref[:, i:i + 1] = carry_out[i]

    bands(0, SB_STATIC_BANDS, True)

    def any_row_live():
        return jnp.max(carry_ref[:, 0:n_sub]) >= F32_EXP_UNDERFLOW

    def unfinished(state):
        d, live = state
        return jnp.logical_and(d <= first_blk + n_sub - 1, live)

    def body(state):
        d, _ = state
        bands(d, 1, False)
        return d + 1, any_row_live()

    lax.while_loop(unfinished, body, (jnp.int32(SB_STATIC_BANDS), any_row_live()))
    o_ref[0] = acc_ref[...].astype(o_ref.dtype)


def sb_attention(q, kv, *, tq):
    b, s, width = q.shape
    return pl.pallas_call(
        _sb_kernel,
        grid=(b, HEADS, s // tq),
        in_specs=[
            pl.BlockSpec((1, tq, LANES), lambda bi, h, i: (bi, i, h)),
            pl.BlockSpec((1, s, LANES), lambda bi, h, i: (bi, 0, h)),
            pl.BlockSpec((1, s, LANES), lambda bi, h, i: (bi, 0, HEADS + h)),
        ],
        out_specs=pl.BlockSpec((1, tq, LANES), lambda bi, h, i: (bi, i, h)),
        out_shape=jax.ShapeDtypeStruct((b, s, width), BF16),
        scratch_shapes=[pltpu.VMEM((tq, LANES), F32), pltpu.VMEM((SB_BLK, LANES), F32)],
        compiler_params=_cparams(("parallel", "parallel", "arbitrary")),
        name="sb_attention",
    )(q, kv, kv)


def kernel(x, p, ln_mix, ln_ffn, ln_ple, gdn_w_in, gdn_conv, gdn_a_log, gdn_dt_bias, gdn_norm,
           gdn_w_out, kv_norm, w_kv, k_norm, sb_w_q, sb_q_norm, sb_w_out, ffn_w_in, ffn_w_out,
           ple_w_proj, ple_w_gate):
    b, s, d = x.shape
    depth = p.shape[0]
    n_a = gdn_w_in.shape[0]
    m = b * s
    width = HEADS * LANES

    bf = lambda w: w.astype(BF16)
    gains = lambda g: g.reshape(g.shape[0], 1, g.shape[-1])
    gdn_w_in_b, gdn_w_out_b, sb_w_q_b, sb_w_out_b = bf(gdn_w_in), bf(gdn_w_out), bf(sb_w_q), bf(sb_w_out)
    ffn_w_in_b, ffn_w_out_b, ple_w_proj_b, ple_w_gate_b = bf(ffn_w_in), bf(ffn_w_out), bf(ple_w_proj), bf(ple_w_gate)
    n_ab = gdn_w_in.shape[2] - 4 * width
    gdn_w_ab_b = bf(jnp.pad(gdn_w_in[:, :, 4 * width:], ((0, 0), (0, 0), (0, LANES - n_ab))))
    ln_mix_g, ln_ffn_g, ln_ple_g, sb_q_norm_g = gains(ln_mix), gains(ln_ffn), gains(ln_ple), gains(sb_q_norm)
    p_rows = p.reshape(depth, m, p.shape[-1])

    h = x.reshape(m, d)
    kv = None
    for i in range(depth):
        if i < n_a:
            proj_main, proj_ab = proj_gdn(h, ln_mix_g, gdn_w_in_b, gdn_w_ab_b, gdn_conv, tm=2 * TM, layer=i, seq_len=s)
            o = gdn_core(proj_main.reshape(b, s, 4 * width), proj_ab.reshape(b, s, LANES),
                         gdn_a_log[i], gdn_dt_bias[i], gdn_norm[i])
            w_out, mixer_layer = gdn_w_out_b, i
        else:
            j = i - n_a
            q = proj(h, ln_mix_g, sb_w_q_b, sb_q_norm_g, tm=TM, g_layer=i, w_layer=j, n_normed=width,
                     out_scale=LANES ** -0.5)
            o = sb_attention(q.reshape(b, s, width), kv, tq=SB_TQ)
            w_out, mixer_layer = sb_w_out_b, j
        h = post_mixer(h, o.reshape(m, width), p_rows, w_out, ln_ffn_g, ffn_w_in_b, ffn_w_out_b, ln_ple_g,
                       ple_w_proj_b, ple_w_gate_b, tm=TM, layer=i, mixer_layer=mixer_layer)
        if i == n_a - 1:
            kv = proj(h, kv_norm.reshape(1, d), bf(w_kv), k_norm.reshape(1, LANES), tm=TM,
                      n_normed=width).reshape(b, s, 2 * width)
    return h.reshape(b, s, d)
```

```python
import functools

import jax
import jax.numpy as jnp
from jax import lax
from jax.experimental import pallas as pl
from jax.experimental.pallas import tpu as pltpu

EPS = 1e-6
LANES = 128
HEADS = 8
CONV_W = 4
HALO = 16
VMEM_LIMIT = 56 * 1024 * 1024
TM = 512
FFN_CHUNK = 256

F32 = jnp.float32
BF16 = jnp.bfloat16


def _cparams(sem):
    return pltpu.CompilerParams(dimension_semantics=sem, vmem_limit_bytes=VMEM_LIMIT)


def _dot(a, b):
    return jnp.dot(a, b, preferred_element_type=F32)


def _dot_nt(a, b):
    return lax.dot_general(a, b, (((1,), (1,)), ((), ())), preferred_element_type=F32)


def _rms(x, g):
    return x * lax.rsqrt(jnp.mean(x * x, axis=-1, keepdims=True) + EPS) * g


def _sigmoid(x):
    return 1.0 / (1.0 + jnp.exp(-x))


def _resident(arr, layer=None, cols=None):
    shape = arr.shape if layer is None else arr.shape[1:]
    if cols is not None:
        shape = shape[:-1] + (cols,)
    zeros = (0,) * len(shape)
    if layer is None:
        return pl.BlockSpec(shape, lambda i: zeros, pipeline_mode=pl.Buffered(1))
    return pl.BlockSpec((None,) + shape, lambda i: (layer,) + zeros, pipeline_mode=pl.Buffered(1))


def _proj_pair_kernel(x_ref, ga_ref, wa_ref, ha_ref, gb_ref, wb_ref, hb_ref, oa_ref, ob_ref, *,
                      na_normed, nb_normed, b_scale):
    _proj_kernel(x_ref, ga_ref, wa_ref, ha_ref, oa_ref, n_normed=na_normed, out_scale=1.0)
    _proj_kernel(x_ref, gb_ref, wb_ref, hb_ref, ob_ref, n_normed=nb_normed, out_scale=b_scale)


def proj_pair(x, ga, wa, ha, gb, wb, hb, *, tm, gb_layer, wb_layer, na_normed, nb_normed, b_scale):
    m, d = x.shape
    na, nb = wa.shape[-1], wb.shape[-1]
    kern = functools.partial(_proj_pair_kernel, na_normed=na_normed, nb_normed=nb_normed, b_scale=b_scale)
    return pl.pallas_call(
        kern,
        grid=(m // tm,),
        in_specs=[pl.BlockSpec((tm, d), lambda i: (i, 0)), _resident(ga), _resident(wa), _resident(ha),
                  _resident(gb, gb_layer), _resident(wb, wb_layer), _resident(hb, wb_layer)],
        out_specs=[pl.BlockSpec((tm, na), lambda i: (i, 0)), pl.BlockSpec((tm, nb), lambda i: (i, 0))],
        out_shape=[jax.ShapeDtypeStruct((m, na), BF16), jax.ShapeDtypeStruct((m, nb), BF16)],
        compiler_params=_cparams(("parallel",)),
        name="proj_pair",
    )(x, ga, wa, ha, gb, wb, hb)


def _proj_kernel(x_ref, g_ref, w_ref, hg_ref, o_ref, *, n_normed, out_scale):
    xn = _rms(x_ref[...], g_ref[...]).astype(BF16)
    hg = hg_ref[...]
    n = w_ref.shape[1]
    step = 2 * LANES
    y_next = _dot(xn, w_ref[:, 0:step])
    for lo in range(0, n, step):
        y = y_next
        if lo + step < n:
            y_next = _dot(xn, w_ref[:, lo + step:lo + 2 * step])
        for c in range(step // LANES):
            col = lo + c * LANES
            yc = y[:, c * LANES:(c + 1) * LANES]
            if col < n_normed:
                yc = _rms(yc, hg) * out_scale
            o_ref[:, col:col + LANES] = yc.astype(o_ref.dtype)


def proj(x, g, w, head_g, *, tm, g_layer=None, w_layer=None, n_normed=0, out_scale=1.0):
    m, d = x.shape
    n = w.shape[-1]
    kern = functools.partial(_proj_kernel, n_normed=n_normed, out_scale=out_scale)
    return pl.pallas_call(
        kern,
        grid=(m // tm,),
        in_specs=[pl.BlockSpec((tm, d), lambda i: (i, 0)), _resident(g, g_layer), _resident(w, w_layer),
                  _resident(head_g, w_layer)],
        out_specs=pl.BlockSpec((tm, n), lambda i: (i, 0)),
        out_shape=jax.ShapeDtypeStruct((m, n), BF16),
        compiler_params=_cparams(("parallel",)),
        name="proj",
    )(x, g, w, head_g)


def _proj_gdn_kernel(x_ref, xh_ref, g_ref, w_ref, wab_ref, conv_ref, o_ref, ab_ref, *, tiles_per_seq):
    i = pl.program_id(0)
    width = HEADS * LANES
    g = g_ref[...]
    xn = _rms(x_ref[...], g).astype(BF16)
    xa = jnp.concatenate([_rms(xh_ref[...], g).astype(BF16), xn], axis=0)
    seq_start = i % tiles_per_seq == 0
    ones = jnp.ones((LANES, LANES), BF16)
    step = 4 * LANES
    chunks = list(range(0, 3 * width, step))
    y_next = _dot(xa, w_ref[:, 0:step])
    for lo in chunks:
        y = jnp.concatenate([jnp.where(seq_start, 0.0, y_next[:HALO]), y_next[HALO:]], axis=0)
        if lo + step < 3 * width:
            y_next = _dot(xa, w_ref[:, lo + step:lo + 2 * step])
        tap = lambda k: conv_ref[CONV_W - 1 - k:CONV_W - k, lo:lo + step]
        y1 = pltpu.roll(y, 1, 0)
        acc = (tap(0) * y + tap(1) * y1 + pltpu.roll(tap(2) * y + tap(3) * y1, 2, 0))[HALO:]
        act = acc * _sigmoid(acc)
        for c in range(step // LANES):
            col = lo + c * LANES
            a = act[:, c * LANES:(c + 1) * LANES]
            if col < 2 * width:
                a = a * lax.rsqrt(_dot((a * a).astype(BF16), ones) + EPS)
                if col < width:
                    a = a * (LANES ** -0.5)
            o_ref[:, col:col + LANES] = a.astype(o_ref.dtype)
    for lo in range(3 * width, 4 * width, step):
        y = _dot(xn, w_ref[:, lo:lo + step])
        o_ref[:, lo:lo + step] = (y * _sigmoid(y)).astype(o_ref.dtype)
    ab_ref[...] = _dot(xn, wab_ref[...])


def proj_gdn(x, g, w_in, w_ab, conv_w, *, tm, layer, seq_len):
    m, d = x.shape
    n = 4 * HEADS * LANES
    halo_blocks = tm // HALO
    kern = functools.partial(_proj_gdn_kernel, tiles_per_seq=seq_len // tm)
    return pl.pallas_call(
        kern,
        grid=(m // tm,),
        in_specs=[
            pl.BlockSpec((tm, d), lambda i: (i, 0)),
            pl.BlockSpec((HALO, d), lambda i: (jnp.maximum(i * halo_blocks - 1, 0), 0)),
            _resident(g, layer), _resident(w_in, layer, cols=n), _resident(w_ab, layer),
            _resident(conv_w, layer),
        ],
        out_specs=[pl.BlockSpec((tm, n), lambda i: (i, 0)), pl.BlockSpec((tm, LANES), lambda i: (i, 0))],
        out_shape=[jax.ShapeDtypeStruct((m, n), BF16), jax.ShapeDtypeStruct((m, LANES), F32)],
        compiler_params=_cparams(("parallel",)),
        name="proj_gdn",
    )(x, x, g, w_in, w_ab, conv_w)


def _post_kernel(h_ref, o_ref, p_ref, wo_ref, gf_ref, w1_ref, w2_ref, gp_ref, wp_ref, wg_ref, out_ref,
                 hn_ref, acc_ref):
    hidden = w2_ref.shape[0]
    h1 = h_ref[...] + _dot(o_ref[...], wo_ref[...])
    hn_ref[...] = _rms(h1, gf_ref[...]).astype(BF16)
    acc_ref[...] = h1

    def gate_and_linear(lo):
        hn = hn_ref[...]
        return (_dot(hn, w1_ref[:, lo:lo + FFN_CHUNK]),
                _dot(hn, w1_ref[:, hidden + lo:hidden + lo + FFN_CHUNK]))

    a, b = gate_and_linear(0)
    for lo in range(0, hidden, FFN_CHUNK):
        act = (a * _sigmoid(a) * b).astype(BF16)
        if lo + FFN_CHUNK < hidden:
            a, b = gate_and_linear(lo + FFN_CHUNK)
        acc_ref[...] += _dot(act, w2_ref[lo:lo + FFN_CHUNK, :])
    h2 = acc_ref[...]
    gate = _sigmoid(_dot(_rms(h2, gp_ref[...]).astype(BF16), wg_ref[...]))
    emb = _dot(p_ref[...].astype(BF16), wp_ref[...])
    out_ref[...] = h2 + emb * gate


def post_mixer(h, o, p, w_out, g_ffn, w1, w2, g_ple, wp, wg, *, tm, layer, mixer_layer):
    m, d = h.shape
    return pl.pallas_call(
        _post_kernel,
        grid=(m // tm,),
        in_specs=[
            pl.BlockSpec((tm, d), lambda i: (i, 0)),
            pl.BlockSpec((tm, o.shape[1]), lambda i: (i, 0)),
            pl.BlockSpec((None, tm, p.shape[2]), lambda i: (layer, i, 0)),
            _resident(w_out, mixer_layer), _resident(g_ffn, layer), _resident(w1, layer), _resident(w2, layer),
            _resident(g_ple, layer), _resident(wp, layer), _resident(wg, layer),
        ],
        out_specs=pl.BlockSpec((tm, d), lambda i: (i, 0)),
        out_shape=jax.ShapeDtypeStruct((m, d), F32),
        scratch_shapes=[pltpu.VMEM((tm, d), BF16), pltpu.VMEM((tm, d), F32)],
        compiler_params=_cparams(("parallel",)),
        name="post_mixer",
    )(h, o, p, w_out, g_ffn, w1, w2, g_ple, wp, wg)


TRI_BASE_LOG2 = 4
GDN_CHUNK = LANES
GDN_CHUNKS_PER_STEP = 8


def _tri_masks(row, col):
    c = row.shape[0]
    eye = (row == col).astype(F32)
    diag = (row >> TRI_BASE_LOG2) == (col >> TRI_BASE_LOG2)
    levels = []
    sh = TRI_BASE_LOG2
    while (1 << sh) < c:
        rb = row >> sh
        levels.append(((rb & 1) == 1) & ((col >> sh) == rb - 1))
        sh += 1
    return eye, diag, levels


def _bdot(a, b):
    return _dot(a.astype(BF16), b.astype(BF16))


def _split_bf16(x):
    hi = x.astype(BF16)
    return hi, (x - hi.astype(F32)).astype(BF16)


def _gdn_kernel(qkv_ref, gate_ref, ab_ref, alog_ref, dtb_ref, ng_ref, o_ref, state_ref):
    i = pl.program_id(1)
    c = GDN_CHUNK
    n_ck = qkv_ref.shape[1] // c
    width = HEADS * LANES
    heads = range(HEADS)
    items = [(ck, h) for ck in range(n_ck) for h in heads]
    n = range(len(items))

    @pl.when(i == 0)
    def _():
        state_ref[...] = jnp.zeros_like(state_ref)

    row = lax.broadcasted_iota(jnp.int32, (c, c), 0)
    col = lax.broadcasted_iota(jnp.int32, (c, c), 1)
    causal = row >= col
    strict = row > col
    eye, diag, levels = _tri_masks(row, col)
    tri_incl = causal.astype(BF16)

    gc_all, gc_t, beta_all, eg_all, egl_all, gl_all = [], [], [], [], [], []
    for ck in range(n_ck):
        ab = ab_ref[0, ck * c:(ck + 1) * c, :]
        x_dt = ab + dtb_ref[...]
        softplus = jnp.maximum(x_dt, 0.0) + jnp.log(1.0 + jnp.exp(-jnp.abs(x_dt)))
        g_hi, g_lo = _split_bf16(-jnp.exp(alog_ref[...]) * softplus)
        gc = _dot(tri_incl, g_hi) + _dot(tri_incl, g_lo)
        g_last = gc[c - 1:c, :]
        gc_all.append(gc)
        gc_t.append(gc.T)
        beta_all.append(_sigmoid(ab))
        eg_all.append(jnp.exp(gc))
        egl_all.append(jnp.exp(g_last - gc))
        gl_all.append(jnp.exp(g_last))

    def section(sec, ck, h):
        lo = sec * width + h * LANES
        return qkv_ref[0, ck * c:(ck + 1) * c, lo:lo + LANES]

    qb = [section(0, ck, h) for ck, h in items]
    kb = [section(1, ck, h) for ck, h in items]
    q = [qb[j].astype(F32) for j in n]
    k = [kb[j].astype(F32) for j in n]
    v = [section(2, ck, h).astype(F32) for ck, h in items]
    decay = []
    for ck, h in items:
        diff = jnp.where(causal, gc_all[ck][:, h:h + 1] - gc_t[ck][h:h + 1, :], 0.0)
        decay.append(jnp.where(causal, jnp.exp(diff), 0.0))
    beta = [beta_all[ck][:, HEADS + h:HEADS + h + 1] for ck, h in items]
    eg = [eg_all[ck][:, h:h + 1] for ck, h in items]

    qkk = [_dot_nt(jnp.concatenate([qb[j], kb[j]], axis=0), kb[j]) for j in n]
    qk = [(qkk[j][:c] * decay[j]).astype(BF16) for j in n]
    a_low = [jnp.where(strict, beta[j] * qkk[j][c:] * decay[j], 0.0) for j in n]

    pw = [jnp.where(diag, -a_low[j], 0.0) for j in n]
    t = [eye + pw[j] for j in n]
    for _ in range(TRI_BASE_LOG2 - 1):
        pw = [_bdot(pw[j], pw[j]) for j in n]
        t = [t[j] + _bdot(t[j], pw[j]) for j in n]
    for lvl in levels:
        y = [_bdot(jnp.where(lvl, a_low[j], 0.0), t[j]) for j in n]
        t = [t[j] - _bdot(t[j], y[j]) for j in n]

    rhs = [jnp.concatenate([v[j] * beta[j], k[j] * (beta[j] * eg[j])], axis=1) for j in n]
    sol = [_bdot(t[j], rhs[j]) for j in n]
    lhs1 = [jnp.concatenate([sol[j][:, LANES:], q[j] * eg[j]], axis=0).astype(BF16) for j in n]
    lhs2 = [jnp.concatenate([qk[j], (k[j] * egl_all[ck][:, h:h + 1]).T.astype(BF16)], axis=0)
            for j, (ck, h) in enumerate(items)]

    ng = ng_ref[...]
    state = [state_ref[h] for h in heads]
    for ck in range(n_ck):
        js = [ck * HEADS + h for h in heads]
        sb = [state[h].astype(BF16) for h in heads]
        r1 = [_dot(lhs1[js[h]], sb[h]) for h in heads]
        vb = [(sol[js[h]][:, :LANES] - r1[h][:c]).astype(BF16) for h in heads]
        r2 = [_dot(lhs2[js[h]], vb[h]) for h in heads]
        state = [state[h] * gl_all[ck][:, h:h + 1] + r2[h][c:] for h in heads]
        for h in heads:
            o = r1[h][c:] + r2[h][:c]
            gate = gate_ref[0, ck * c:(ck + 1) * c, h * LANES:(h + 1) * LANES].astype(F32)
            scale = lax.rsqrt(jnp.mean(o * o, axis=-1, keepdims=True) + EPS)
            o_ref[0, ck * c:(ck + 1) * c, h * LANES:(h + 1) * LANES] = (o * scale * ng * gate).astype(o_ref.dtype)
    for h in heads:
        state_ref[h] = state[h]


def gdn_core(proj, ab, a_log, dt_bias, norm_g):
    b, s, _ = proj.shape
    width = HEADS * LANES
    chunk = GDN_CHUNK * GDN_CHUNKS_PER_STEP
    pad = lambda v: jnp.zeros((1, LANES), F32).at[0, :HEADS].set(v)
    return pl.pallas_call(
        _gdn_kernel,
        grid=(b, s // chunk),
        in_specs=[
            pl.BlockSpec((1, chunk, 3 * width), lambda bi, i: (bi, i, 0)),
            pl.BlockSpec((1, chunk, width), lambda bi, i: (bi, i, 3)),
            pl.BlockSpec((1, chunk, LANES), lambda bi, i: (bi, i, 0)),
            pl.BlockSpec((1, LANES), lambda bi, i: (0, 0)),
            pl.BlockSpec((1, LANES), lambda bi, i: (0, 0)),
            pl.BlockSpec((1, LANES), lambda bi, i: (0, 0)),
        ],
        out_specs=pl.BlockSpec((1, chunk, width), lambda bi, i: (bi, i, 0)),
        out_shape=jax.ShapeDtypeStruct((b, s, width), BF16),
        scratch_shapes=[pltpu.VMEM((HEADS, LANES, LANES), F32)],
        compiler_params=_cparams(("parallel", "arbitrary")),
        name="gdn_core",
    )(proj, proj, ab, pad(a_log), pad(dt_bias), norm_g.reshape(1, LANES))


SB_BLK = LANES
SB_TQ = 4096
SB_STATIC_BANDS = 3
F32_EXP_UNDERFLOW = -104.0
NEG_BIG = -1e30


def _sb_bands(qs, kbs, vbs, upper, carries, valid, diag_mask):
    bands, subs = range(len(kbs)), range(len(qs))
    z = [[_dot_nt(qs[i], kbs[d][i]) for i in subs] for d in bands]
    log_sig, log_not = [], []
    for d in bands:
        ls, ln = [], []
        for i in subs:
            zm = jnp.where(diag_mask, z[d][i], NEG_BIG) if (d == 0 and diag_mask is not None) else z[d][i]
            nz = -zm
            nsp = jnp.minimum(nz, 0.0) - jnp.log(1.0 + jnp.exp(jnp.minimum(zm, nz)))
            ls.append(zm + nsp)
            ln.append(nsp)
        log_sig.append(ls)
        log_not.append(ln)
    suffix = [[_dot(log_not[d][i].astype(BF16), upper) for i in subs] for d in bands]
    wgt = [[] for _ in subs]
    carry_out = []
    for i in subs:
        carry = carries[i]
        for d in bands:
            if valid[d][i] is not None:
                carry = jnp.where(valid[d][i], carry, NEG_BIG)
            wgt[i].append(jnp.exp(log_sig[d][i] + suffix[d][i] + carry).astype(BF16))
            carry = carry + (suffix[d][i][:, 0:1] + log_not[d][i][:, 0:1])
        carry_out.append(carry)
    out = [_dot(jnp.concatenate(wgt[i], axis=1), jnp.concatenate([vbs[d][i] for d in bands], axis=0))
           for i in subs]
    return out, carry_out


def _sb_kernel(q_ref, k_ref, v_ref, o_ref, acc_ref, carry_ref):
    gi = pl.program_id(2)
    n_sub = q_ref.shape[1] // SB_BLK
    first_blk = gi * n_sub
    subs = range(n_sub)
    row = lax.broadcasted_iota(jnp.int32, (SB_BLK, SB_BLK), 0)
    col = lax.broadcasted_iota(jnp.int32, (SB_BLK, SB_BLK), 1)
    upper = (row > col).astype(BF16)

    def rows(i):
        return slice(i * SB_BLK, (i + 1) * SB_BLK)

    def bands(d0, n_bands, first):
        qs = [q_ref[0, rows(i), :] for i in subs]
        kbs, vbs, valid = [], [], []
        for d in range(n_bands):
            blk = [first_blk + i - (d0 + d) for i in subs]
            start = [pl.multiple_of(jnp.maximum(blk[i], 0) * SB_BLK, SB_BLK) for i in subs]
            kbs.append([k_ref[0, pl.ds(start[i], SB_BLK), :] for i in subs])
            vbs.append([v_ref[0, pl.ds(start[i], SB_BLK), :] for i in subs])
            valid.append([None if (first and d == 0) else blk[i] >= 0 for i in subs])
        carries = [0.0 if first else carry_ref[:, i:i + 1] for i in subs]
        out, carry_out = _sb_bands(qs, kbs, vbs, upper, carries, valid, col < row if first else None)
        for i in subs:
            if first:
                acc_ref[rows(i), :] = out[i]
            else:
                acc_ref[rows(i), :] += out[i]
            carry_ref[:, i:i + 1] = carry_out[i]

    bands(0, SB_STATIC_BANDS, True)

    def any_row_live():
        return jnp.max(carry_ref[:, 0:n_sub]) >= F32_EXP_UNDERFLOW

    def unfinished(state):
        d, live = state
        return jnp.logical_and(d <= first_blk + n_sub - 1, live)

    def body(state):
        d, _ = state
        bands(d, 1, False)
        return d + 1, any_row_live()

    lax.while_loop(unfinished, body, (jnp.int32(SB_STATIC_BANDS), any_row_live()))
    o_ref[0] = acc_ref[...].astype(o_ref.dtype)


def sb_attention(q, kv, *, tq):
    b, s, width = q.shape
    return pl.pallas_call(
        _sb_kernel,
        grid=(b, HEADS, s // tq),
        in_specs=[
            pl.BlockSpec((1, tq, LANES), lambda bi, h, i: (bi, i, h)),
            pl.BlockSpec((1, s, LANES), lambda bi, h, i: (bi, 0, h)),
            pl.BlockSpec((1, s, LANES), lambda bi, h, i: (bi, 0, HEADS + h)),
        ],
        out_specs=pl.BlockSpec((1, tq, LANES), lambda bi, h, i: (bi, i, h)),
        out_shape=jax.ShapeDtypeStruct((b, s, width), BF16),
        scratch_shapes=[pltpu.VMEM((tq, LANES), F32), pltpu.VMEM((SB_BLK, LANES), F32)],
        compiler_params=_cparams(("parallel", "parallel", "arbitrary")),
        name="sb_attention",
    )(q, kv, kv)


def kernel(x, p, ln_mix, ln_ffn, ln_ple, gdn_w_in, gdn_conv, gdn_a_log, gdn_dt_bias, gdn_norm,
           gdn_w_out, kv_norm, w_kv, k_norm, sb_w_q, sb_q_norm, sb_w_out, ffn_w_in, ffn_w_out,
           ple_w_proj, ple_w_gate):
    b, s, d = x.shape
    depth = p.shape[0]
    n_a = gdn_w_in.shape[0]
    m = b * s
    width = HEADS * LANES

    bf = lambda w: w.astype(BF16)
    gains = lambda g: g.reshape(g.shape[0], 1, g.shape[-1])
    gdn_w_in_b, gdn_w_out_b, sb_w_q_b, sb_w_out_b = bf(gdn_w_in), bf(gdn_w_out), bf(sb_w_q), bf(sb_w_out)
    ffn_w_in_b, ffn_w_out_b, ple_w_proj_b, ple_w_gate_b = bf(ffn_w_in), bf(ffn_w_out), bf(ple_w_proj), bf(ple_w_gate)
    n_ab = gdn_w_in.shape[2] - 4 * width
    gdn_w_ab_b = bf(jnp.pad(gdn_w_in[:, :, 4 * width:], ((0, 0), (0, 0), (0, LANES - n_ab))))
    ln_mix_g, ln_ffn_g, ln_ple_g, sb_q_norm_g = gains(ln_mix), gains(ln_ffn), gains(ln_ple), gains(sb_q_norm)
    p_rows = p.reshape(depth, m, p.shape[-1])

    h = x.reshape(m, d)
    kv = None
    for i in range(depth):
        if i < n_a:
            proj_main, proj_ab = proj_gdn(h, ln_mix_g, gdn_w_in_b, gdn_w_ab_b, gdn_conv, tm=2 * TM, layer=i, seq_len=s)
            o = gdn_core(proj_main.reshape(b, s, 4 * width), proj_ab.reshape(b, s, LANES),
                         gdn_a_log[i], gdn_dt_bias[i], gdn_norm[i])
            w_out, mixer_layer = gdn_w_out_b, i
        else:
            j = i - n_a
            if j > 0:
                q = proj(h, ln_mix_g, sb_w_q_b, sb_q_norm_g, tm=TM, g_layer=i, w_layer=j, n_normed=width,
                         out_scale=LANES ** -0.5)
            o = sb_attention(q.reshape(b, s, width), kv, tq=SB_TQ)
            w_out, mixer_layer = sb_w_out_b, j
        h = post_mixer(h, o.reshape(m, width), p_rows, w_out, ln_ffn_g, ffn_w_in_b, ffn_w_out_b, ln_ple_g,
                       ple_w_proj_b, ple_w_gate_b, tm=TM, layer=i, mixer_layer=mixer_layer)
        if i == n_a - 1:
            kv, q = proj_pair(h, kv_norm.reshape(1, d), bf(w_kv), k_norm.reshape(1, LANES),
                              ln_mix_g, sb_w_q_b, sb_q_norm_g, tm=TM, gb_layer=n_a, wb_layer=0,
                              na_normed=width, nb_normed=width, b_scale=LANES ** -0.5)
            kv = kv.reshape(b, s, 2 * width)
    return h.reshape(b, s, d)
```
